```python
import jax, jax.numpy as jnp
from jax import lax
import numpy as np

D_MODEL = 1024
BATCH = 2
SEQ = 16384
DEPTH = 2
DEC_BATCH = 32
DEC_SEQ = 64
PAST_LEN = 1024

CHUNK = 64
N_MEM = 256
EPS = 1e-6
MLA_HEADS = 8
Q_RANK = 256
KV_RANK = 128
NOPE_DIM = 64
ROPE_DIM = 32
QK_DIM = NOPE_DIM + ROPE_DIM
V_DIM = 64
ROPE_THETA = 10000.0
MLA_SCALE = QK_DIM ** -0.5
Q_BLOCK = 128
ML_HEADS = 4
ML_DH = 128
ML_WIDTH = ML_HEADS * ML_DH
CONV_W = 4
XA_HEADS = 4
XA_DH = D_MODEL // XA_HEADS
XA_SCALE = XA_DH ** -0.5
D_FF = 4 * D_MODEL
OFF_QA = 0
OFF_KVA = OFF_QA + Q_RANK
OFF_KR = OFF_KVA + KV_RANK
OFF_MQK = OFF_KR + ROPE_DIM
OFF_MV = OFF_MQK + 2 * ML_WIDTH
OFF_MI = OFF_MV + ML_WIDTH
OFF_MF = OFF_MI + ML_HEADS
OFF_MO = OFF_MF + ML_HEADS
IN_COLS = OFF_MO + ML_WIDTH

kernel_name = 'hybrid_mla_mlstm_stream_step'

f32 = jnp.float32


def rmsnorm(x, g):
    x32 = x.astype(f32)
    y = x32 * lax.rsqrt(jnp.mean(x32 * x32, axis=-1, keepdims=True) + EPS)
    return (y * g.astype(f32)).astype(x.dtype)


def apply_rope(x, pos):
    half = ROPE_DIM // 2
    inv_freq = ROPE_THETA ** (-jnp.arange(half, dtype=f32) / half)
    ang = pos.astype(f32)[:, None] * inv_freq[None, :]
    ang = ang.reshape(ang.shape[:1] + (1,) * (x.ndim - 3) + (half,))
    cos, sin = jnp.cos(ang), jnp.sin(ang)
    x32 = x.astype(f32)
    x1, x2 = x32[..., :half], x32[..., half:]
    return jnp.concatenate([x1 * cos - x2 * sin, x1 * sin + x2 * cos], axis=-1).astype(x.dtype)


def causal_conv(x, buf, w, b):
    S = x.shape[1]
    xp = jnp.concatenate([buf.astype(x.dtype), x], axis=1)
    y = b + sum(xp[:, j:j + S] * w[j] for j in range(CONV_W))
    return jax.nn.silu(y), xp[:, S:]


def softmax_attend(q, k, v, mask, scale):
    s = jnp.einsum('bqhd,bkhd->bhqk', q.astype(f32), k.astype(f32)) * scale
    if mask is not None:
        s = jnp.where(mask, s, -jnp.inf)
    p = jax.nn.softmax(s, axis=-1)
    return jnp.einsum('bhqk,bkhd->bqhd', p, v.astype(f32)).astype(v.dtype)


def split_projections(p, pos, conv_buf, P):
    B, S, _ = p.shape
    q_lat = rmsnorm(p[..., OFF_QA:OFF_KVA], P['g_qa'])
    q = (q_lat @ P['w_q_up']).reshape(B, S, MLA_HEADS, QK_DIM)
    q = jnp.concatenate([q[..., :NOPE_DIM], apply_rope(q[..., NOPE_DIM:], pos)], axis=-1)
    q = rmsnorm(q, P['g_qnorm'])
    c_kv = rmsnorm(p[..., OFF_KVA:OFF_KR], P['g_kva'])
    k_rope = apply_rope(p[..., OFF_KR:OFF_MQK], pos)
    qk, new_buf = causal_conv(p[..., OFF_MQK:OFF_MV], conv_buf, P['w_conv'], P['b_conv'])
    mq = qk[..., :ML_WIDTH].reshape(B, S, ML_HEADS, ML_DH)
    mk = qk[..., ML_WIDTH:].reshape(B, S, ML_HEADS, ML_DH) * (ML_DH ** -0.5)
    mv = p[..., OFF_MV:OFF_MI].reshape(B, S, ML_HEADS, ML_DH)
    ig = (p[..., OFF_MI:OFF_MF] + P['b_igate']).astype(f32)
    logf = jax.nn.log_sigmoid((p[..., OFF_MF:OFF_MO] + P['b_fgate']).astype(f32))
    og = jax.nn.sigmoid(p[..., OFF_MO:IN_COLS])
    return q, c_kv, k_rope, mq, mk, mv, ig, logf, og, new_buf


def mla_kv(c_kv, k_rope, P):
    B, T, _ = c_kv.shape
    kv = (c_kv @ P['w_kv_up']).reshape(B, T, MLA_HEADS, NOPE_DIM + V_DIM)
    k_r = jnp.broadcast_to(k_rope[:, :, None, :], (B, T, MLA_HEADS, ROPE_DIM))
    k = rmsnorm(jnp.concatenate([kv[..., :NOPE_DIM], k_r], axis=-1), P['g_knorm'])
    return k, kv[..., NOPE_DIM:]


def mla_prompt(q, k, v):
    B, S = q.shape[:2]
    nb = S // Q_BLOCK
    qb = jnp.moveaxis(q.reshape(B, nb, Q_BLOCK, MLA_HEADS, QK_DIM), 1, 0)
    k_chunk = jnp.arange(S) // CHUNK

    def block(args):
        qi, bi = args
        q_chunk = (bi * Q_BLOCK + jnp.arange(Q_BLOCK)) // CHUNK
        mask = k_chunk[None, :] <= q_chunk[:, None]
        return softmax_attend(qi, k, v, mask, MLA_SCALE)

    o = lax.map(block, (qb, jnp.arange(nb)))
    return jnp.moveaxis(o, 0, 1).reshape(B, S, MLA_HEADS * V_DIM)


def mlstm_chunk(state, q, k, v, ig, logf):
    C, n, m = state
    q, k, v = (jnp.swapaxes(a, 1, 2) for a in (q, k, v))
    ig, logf = jnp.swapaxes(ig, 1, 2), jnp.swapaxes(logf, 1, 2)
    L = q.shape[2]
    b = jnp.cumsum(logf, axis=-1)
    causal = jnp.tril(jnp.ones((L, L), dtype=bool))
    d = jnp.where(causal, b[..., :, None] - b[..., None, :] + ig[..., None, :], -jnp.inf)
    inter = b + m[..., None]
    m_t = jnp.maximum(inter, jnp.max(d, axis=-1))
    w = jnp.exp(d - m_t[..., None])
    a_inter = jnp.exp(inter - m_t)
    sqk = jnp.einsum('bhtd,bhsd->bhts', q, k) * w
    num = a_inter[..., None] * jnp.einsum('bhtd,bhde->bhte', q, C) + jnp.einsum('bhts,bhse->bhte', sqk, v)
    qn = a_inter * jnp.einsum('bhtd,bhd->bht', q, n) + jnp.sum(sqk, axis=-1)
    h = num / jnp.maximum(jnp.abs(qn), jnp.exp(-m_t))[..., None]
    b_last = b[..., -1]
    m_end = m_t[..., -1]
    decay = jnp.exp(b_last + m - m_end)
    wk = jnp.exp(b_last[..., None] - b + ig - m_end[..., None])
    C_new = decay[..., None, None] * C + jnp.einsum('bhs,bhsd,bhse->bhde', wk, k, v)
    n_new = decay[..., None] * n + jnp.einsum('bhs,bhsd->bhd', wk, k)
    return jnp.swapaxes(h, 1, 2), (C_new, n_new, m_end)


def mlstm_prompt(mq, mk, mv, ig, logf):
    B, S = mq.shape[:2]
    nc = S // CHUNK

    def to_chunks(a):
        return jnp.moveaxis(a.astype(f32).reshape((B, nc, CHUNK) + a.shape[2:]), 1, 0)

    init = (jnp.zeros((B, ML_HEADS, ML_DH, ML_DH), f32),
            jnp.zeros((B, ML_HEADS, ML_DH), f32),
            jnp.zeros((B, ML_HEADS), f32))

    def step(carry, xs):
        h, carry = mlstm_chunk(carry, *xs)
        return carry, h

    st, h = lax.scan(step, init, tuple(to_chunks(a) for a in (mq, mk, mv, ig, logf)))
    return jnp.moveaxis(h, 0, 1).reshape(B, S, ML_HEADS, ML_DH), st


def mem_kv(mem, P):
    B = mem.shape[0]
    hm = rmsnorm(mem, P['g_mem'])
    k = rmsnorm((hm @ P['w_xk']).reshape(B, N_MEM, XA_HEADS, XA_DH), P['g_xk'])
    v = (hm @ P['w_xv']).reshape(B, N_MEM, XA_HEADS, XA_DH)
    return k, v


def run_layer(x, pos, conv_buf, mlstm_state, past_ckv, past_krope, mem_k, mem_v, P, prompt):
    B, S, _ = x.shape
    p = rmsnorm(x, P['g_mix']) @ P['w_in']
    q, c_kv, k_rope, mq, mk, mv, ig, logf, og, new_buf = split_projections(p, pos, conv_buf, P)
    if prompt:
        k, v = mla_kv(c_kv, k_rope, P)
        a = mla_prompt(q, k, v)
        hm, st = mlstm_prompt(mq, mk, mv, ig, logf)
    else:
        k, v = mla_kv(jnp.concatenate([past_ckv.astype(c_kv.dtype), c_kv], axis=1),
                      jnp.concatenate([past_krope.astype(k_rope.dtype), k_rope], axis=1), P)
        a = softmax_attend(q, k, v, None, MLA_SCALE).reshape(B, S, MLA_HEADS * V_DIM)
        hm, st = mlstm_chunk(tuple(s.astype(f32) for s in mlstm_state),
                             mq.astype(f32), mk.astype(f32), mv.astype(f32), ig, logf)
    hm = rmsnorm(hm.astype(x.dtype), P['g_mhead']).reshape(B, S, ML_WIDTH) * og
    x = x + jnp.concatenate([a, hm], axis=-1) @ P['w_out']
    hx = rmsnorm(x, P['g_xattn'])
    qx = rmsnorm((hx @ P['w_xq']).reshape(B, S, XA_HEADS, XA_DH), P['g_xq'])
    ox = softmax_attend(qx, mem_k, mem_v, None, XA_SCALE).reshape(B, S, D_MODEL)
    x = x + ox @ P['w_xo']
    hf = rmsnorm(x, P['g_mlp'])
    x = x + jnp.square(jax.nn.relu(hf @ P['w_ff1'])) @ P['w_ff2']
    C, n, m = st
    return x, (c_kv, k_rope, C, n, m, new_buf)


def setup_inputs(seed: int = 0) -> dict:
    key = jax.random.key(seed)
    ks = iter(jax.random.split(key, 64))
    L = DEPTH

    def nrm(shape, scale=1.0):
        return jax.random.normal(next(ks), shape, f32) * scale

    def gain(shape):
        return 1.0 + nrm(shape, 0.01)

    return {
        'x_prompt': nrm((BATCH, SEQ, D_MODEL)),
        'x_sample': nrm((DEC_BATCH, DEC_SEQ, D_MODEL)),
        'cache_mla_ckv': nrm((L, DEC_BATCH, PAST_LEN, KV_RANK)),
        'cache_mla_krope': nrm((L, DEC_BATCH, PAST_LEN, ROPE_DIM)),
        'state_mlstm_C': nrm((L, DEC_BATCH, ML_HEADS, ML_DH, ML_DH), 0.05),
        'state_mlstm_n': nrm((L, DEC_BATCH, ML_HEADS, ML_DH), 0.5),
        'state_mlstm_m': nrm((L, DEC_BATCH, ML_HEADS), 0.5),
        'state_mlstm_conv': nrm((L, DEC_BATCH, CONV_W - 1, 2 * ML_WIDTH)),
        'cache_mem_k': nrm((L, DEC_BATCH, N_MEM, XA_HEADS, XA_DH)),
        'cache_mem_v': nrm((L, DEC_BATCH, N_MEM, XA_HEADS, XA_DH)),
        'mem_prompt': nrm((BATCH, N_MEM, D_MODEL)),
        'g_mix': gain((L, D_MODEL)),
        'w_in': nrm((L, D_MODEL, IN_COLS), D_MODEL ** -0.5),
        'g_qa': gain((L, Q_RANK)),
        'w_q_up': nrm((L, Q_RANK, MLA_HEADS * QK_DIM), Q_RANK ** -0.5),
        'g_qnorm': gain((L, QK_DIM)),
        'g_kva': gain((L, KV_RANK)),
        'w_kv_up': nrm((L, KV_RANK, MLA_HEADS * (NOPE_DIM + V_DIM)), KV_RANK ** -0.5),
        'g_knorm': gain((L, QK_DIM)),
        'w_conv': nrm((L, CONV_W, 2 * ML_WIDTH), 0.5),
        'b_conv': nrm((L, 2 * ML_WIDTH), 0.02),
        'b_igate': nrm((L, ML_HEADS), 0.1),
        'b_fgate': jnp.linspace(3.0, 6.0, ML_HEADS, dtype=f32)[None, :] + nrm((L, ML_HEADS), 0.01),
        'g_mhead': gain((L, ML_HEADS, ML_DH)),
        'w_out': nrm((L, D_MODEL, D_MODEL), D_MODEL ** -0.5),
        'g_xattn': gain((L, D_MODEL)),
        'g_mem': gain((L, D_MODEL)),
        'w_xq': nrm((L, D_MODEL, D_MODEL), D_MODEL ** -0.5),
        'w_xk': nrm((L, D_MODEL, D_MODEL), D_MODEL ** -0.5),
        'w_xv': nrm((L, D_MODEL, D_MODEL), D_MODEL ** -0.5),
        'g_xq': gain((L, XA_DH)),
        'g_xk': gain((L, XA_DH)),
        'w_xo': nrm((L, D_MODEL, D_MODEL), D_MODEL ** -0.5),
        'g_mlp': gain((L, D_MODEL)),
        'w_ff1': nrm((L, D_MODEL, D_FF), D_MODEL ** -0.5),
        'w_ff2': nrm((L, D_FF, D_MODEL), D_FF ** -0.5),
    }


def reference(x_prompt, x_sample, cache_mla_ckv, cache_mla_krope, state_mlstm_C, state_mlstm_n,
              state_mlstm_m, state_mlstm_conv, cache_mem_k, cache_mem_v, mem_prompt,
              g_mix, w_in, g_qa, w_q_up, g_qnorm, g_kva, w_kv_up, g_knorm, w_conv, b_conv,
              b_igate, b_fgate, g_mhead, w_out, g_xattn, g_mem, w_xq, w_xk, w_xv, g_xq, g_xk,
              w_xo, g_mlp, w_ff1, w_ff2):
    Bp, Sp, _ = x_prompt.shape
    past = cache_mla_ckv.shape[2]
    pos_p = jnp.arange(Sp)
    pos_s = past + jnp.arange(x_sample.shape[1])
    xp, xs = x_prompt, x_sample
    conv0 = jnp.zeros((Bp, CONV_W - 1, 2 * ML_WIDTH), x_prompt.dtype)
    p_out = [[] for _ in range(8)]
    s_out = [[] for _ in range(6)]
    for l in range(DEPTH):
        P = {'g_mix': g_mix[l], 'w_in': w_in[l], 'g_qa': g_qa[l], 'w_q_up': w_q_up[l],
             'g_qnorm': g_qnorm[l], 'g_kva': g_kva[l], 'w_kv_up': w_kv_up[l], 'g_knorm': g_knorm[l],
             'w_conv': w_conv[l], 'b_conv': b_conv[l], 'b_igate': b_igate[l], 'b_fgate': b_fgate[l],
             'g_mhead': g_mhead[l], 'w_out': w_out[l], 'g_xattn': g_xattn[l], 'g_mem': g_mem[l],
             'w_xq': w_xq[l], 'w_xk': w_xk[l], 'w_xv': w_xv[l], 'g_xq': g_xq[l], 'g_xk': g_xk[l],
             'w_xo': w_xo[l], 'g_mlp': g_mlp[l], 'w_ff1': w_ff1[l], 'w_ff2': w_ff2[l]}
        mk_p, mv_p = mem_kv(mem_prompt, P)
        xp, st_p = run_layer(xp, pos_p, conv0, None, None, None, mk_p, mv_p, P, True)
        for lst, a in zip(p_out, st_p + (mk_p, mv_p)):
            lst.append(a.astype(x_prompt.dtype))
        xs, st_s = run_layer(xs, pos_s, state_mlstm_conv[l],
                             (state_mlstm_C[l], state_mlstm_n[l], state_mlstm_m[l]),
                             cache_mla_ckv[l], cache_mla_krope[l], cache_mem_k[l], cache_mem_v[l], P, False)
        for lst, a in zip(s_out, st_s):
            lst.append(a.astype(x_sample.dtype))
    p_ckv, p_krope, p_C, p_n, p_m, p_conv, p_mem_k, p_mem_v = (jnp.stack(a) for a in p_out)
    s_ckv, s_krope, s_C, s_n, s_m, s_conv = (jnp.stack(a) for a in s_out)
    return (xp, xs, p_ckv, p_krope, p_C, p_n, p_m, p_conv, p_mem_k, p_mem_v,
            s_ckv, s_krope, s_C, s_n, s_m, s_conv)
```

```python
import functools

import jax
import jax.numpy as jnp
import numpy as np
from jax import lax
from jax.experimental import pallas as pl
from jax.experimental.pallas import tpu as pltpu

f32 = jnp.float32
bf16 = jnp.bfloat16

D_MODEL = 1024
CHUNK = 64
EPS = 1e-6
MLA_HEADS = 8
Q_RANK = 256
KV_RANK = 128
NOPE_DIM = 64
ROPE_DIM = 32
QK_DIM = NOPE_DIM + ROPE_DIM
V_DIM = 64
ROPE_THETA = 10000.0
MLA_SCALE = QK_DIM ** -0.5
ML_HEADS = 4
ML_DH = 128
ML_WIDTH = ML_HEADS * ML_DH
CONV_W = 4
XA_HEADS = 4
XA_DH = D_MODEL // XA_HEADS
XA_SCALE = XA_DH ** -0.5
D_FF = 4 * D_MODEL

LANE = 128
SUBLANE = 8
MIB = 1024 * 1024

C_QA = 0
C_KVA = C_QA + Q_RANK
C_KR = C_KVA + KV_RANK
C_MQK = C_KR + LANE
C_MV = C_MQK + 2 * ML_WIDTH
C_MO = C_MV + ML_WIDTH
C_GATE = C_MO + ML_WIDTH
IN_COLS_PACKED = C_GATE + LANE
GATE_F = ML_HEADS

ROW_TILE = 512
ATT_TILE = 1024
ATT_SUB = 256
ML_CHUNK = 256

NT_DIMS = (((1,), (1,)), ((), ()))


def _rms(x, g):
    return x * lax.rsqrt(jnp.mean(x * x, axis=-1, keepdims=True) + EPS) * g


def _dot(a, b):
    return jnp.dot(a, b, preferred_element_type=f32)


def _dot_nt(a, b):
    return lax.dot_general(a, b, NT_DIMS, preferred_element_type=f32)


def _sigmoid(x):
    return 1.0 / (1.0 + jnp.exp(-x))


def _const_spec(shape):
    nd = len(shape)
    return pl.BlockSpec(shape, lambda *_: (0,) * nd, pipeline_mode=pl.Buffered(1))


def _params(semantics, vmem_mib):
    return pltpu.CompilerParams(dimension_semantics=semantics, vmem_limit_bytes=vmem_mib * MIB)


def _inproj_kernel(x_ref, cos_ref, sa_ref, sb_ref, gmix_ref, win_ref, gqa_ref, wq_ref, gqn_ref, gkva_ref,
                   q_ref, ckv_ref, kr_ref, mqk_ref, mv_ref, og_ref, gate_ref):
    x = x_ref[0]
    h = _rms(x, gmix_ref[...])
    p = _dot(h.astype(bf16), win_ref[...])
    cos, sa, sb = cos_ref[...], sa_ref[...], sb_ref[...]

    def rope(t):
        return t * cos + pltpu.roll(t, LANE - ROPE_DIM // 2, 1) * sa + pltpu.roll(t, ROPE_DIM // 2, 1) * sb

    q_lat = _rms(p[:, C_QA:C_KVA], gqa_ref[...])
    qf = _dot(q_lat.astype(bf16), wq_ref[...])
    gq = gqn_ref[...]
    for hd in range(MLA_HEADS):
        t = rope(qf[:, hd * LANE:(hd + 1) * LANE])
        ss = jnp.sum(t * t, axis=-1, keepdims=True) * (1.0 / QK_DIM)
        q_ref[0, hd] = (t * lax.rsqrt(ss + EPS) * gq * MLA_SCALE).astype(bf16)
    ckv_ref[0] = _rms(p[:, C_KVA:C_KR], gkva_ref[...])
    kr_ref[0] = rope(p[:, C_KR:C_MQK])
    mqk_ref[0] = p[:, C_MQK:C_MV]
    mv_ref[0] = p[:, C_MV:C_MO]
    og_ref[0] = p[:, C_MO:C_GATE]
    gate_ref[0] = p[:, C_GATE:IN_COLS_PACKED]


def _inproj(x, tabs, W):
    bv, sv, _ = x.shape
    tm = min(ROW_TILE, sv)
    row = lambda w: pl.BlockSpec((1, tm, w), lambda b, j: (b, j, 0))
    tab = pl.BlockSpec((tm, LANE), lambda b, j: (j, 0))
    sds = jax.ShapeDtypeStruct
    return pl.pallas_call(
        _inproj_kernel,
        grid=(bv, sv // tm),
        in_specs=[row(D_MODEL), tab, tab, tab,
                  _const_spec((1, D_MODEL)), _const_spec((D_MODEL, IN_COLS_PACKED)),
                  _const_spec((1, Q_RANK)), _const_spec((Q_RANK, MLA_HEADS * LANE)),
                  _const_spec((1, LANE)), _const_spec((1, KV_RANK))],
        out_specs=[pl.BlockSpec((1, MLA_HEADS, tm, LANE), lambda b, j: (b, 0, j, 0)),
                   row(KV_RANK), row(LANE), row(2 * ML_WIDTH), row(ML_WIDTH), row(ML_WIDTH), row(LANE)],
        out_shape=[sds((bv, MLA_HEADS, sv, LANE), bf16), sds((bv, sv, KV_RANK), f32), sds((bv, sv, LANE), f32),
                   sds((bv, sv, 2 * ML_WIDTH), f32), sds((bv, sv, ML_WIDTH), f32), sds((bv, sv, ML_WIDTH), f32),
                   sds((bv, sv, LANE), f32)],
        compiler_params=_params(("parallel", "parallel"), 48),
        name="inproj",
    )(x, *tabs, W["g_mix"], W["w_in"], W["g_qa"], W["w_q"], W["g_qn"], W["g_kva"])


def _kvup_kernel(ckv_ref, kr_ref, wk_ref, wv_ref, gk_ref, k_ref, v_ref, *, transposed_v):
    c = ckv_ref[0].astype(bf16)
    kn = _dot(c, wk_ref[...])
    kr = kr_ref[0]
    g = gk_ref[...]
    for hd in range(MLA_HEADS):
        t = kn[:, hd * LANE:(hd + 1) * LANE] + kr
        ss = jnp.sum(t * t, axis=-1, keepdims=True) * (1.0 / QK_DIM)
        k_ref[0, hd] = (t * lax.rsqrt(ss + EPS) * g).astype(bf16)
    if transposed_v:
        v_ref[0] = _dot_nt(wv_ref[...], c).astype(bf16)
    else:
        v_ref[0] = _dot(c, wv_ref[...]).astype(bf16)


def _kvup(ckv, kr, W, transposed_v):
    bv, sv, _ = ckv.shape
    tm = min(ROW_TILE, sv)
    hv = MLA_HEADS * V_DIM
    row = lambda w: pl.BlockSpec((1, tm, w), lambda b, j: (b, j, 0))
    if transposed_v:
        v_spec = pl.BlockSpec((1, hv, tm), lambda b, j: (b, 0, j))
        v_shape = jax.ShapeDtypeStruct((bv, hv, sv), bf16)
        wv = W["w_vT"]
    else:
        v_spec = row(hv)
        v_shape = jax.ShapeDtypeStruct((bv, sv, hv), bf16)
        wv = W["w_v"]
    return pl.pallas_call(
        functools.partial(_kvup_kernel, transposed_v=transposed_v),
        grid=(bv, sv // tm),
        in_specs=[row(KV_RANK), row(LANE), _const_spec((KV_RANK, MLA_HEADS * LANE)), _const_spec(wv.shape),
                  _const_spec((1, LANE))],
        out_specs=[pl.BlockSpec((1, MLA_HEADS, tm, LANE), lambda b, j: (b, 0, j, 0)), v_spec],
        out_shape=[jax.ShapeDtypeStruct((bv, MLA_HEADS, sv, LANE), bf16), v_shape],
        compiler_params=_params(("parallel", "parallel"), 32),
        name="kvup_t" if transposed_v else "kvup",
    )(ckv, kr, W["w_k"], wv, W["g_kn"])


def _mla_prompt_kernel(qi_ref, ki_ref, q_ref, k_ref, vt_ref, o_ref, m_sc, l_sc, acc_sc, *, tile, sub):
    p = pl.program_id(1)
    qi, ki = qi_ref[p], ki_ref[p]
    nsub = tile // sub

    @pl.when(ki == 0)
    def _():
        m_sc[...] = jnp.full(m_sc.shape, -jnp.inf, f32)
        l_sc[...] = jnp.zeros(l_sc.shape, f32)
        acc_sc[...] = jnp.zeros(acc_sc.shape, f32)

    def head_step(hd, diag):
        for qs in range(nsub):
            cols = slice(qs * sub, (qs + 1) * sub)
            kv_len = (qs + 1) * sub if diag else tile
            s = _dot_nt(k_ref[0, hd, 0:kv_len, :], q_ref[0, hd, cols, :])
            if diag:
                kc = lax.broadcasted_iota(jnp.int32, (kv_len, 1), 0) // CHUNK
                qc = (lax.broadcasted_iota(jnp.int32, (1, sub), 1) + qs * sub) // CHUNK
                s = jnp.where(kc <= qc, s, -jnp.inf)
            m_old = m_sc[hd, :, cols]
            m_new = jnp.maximum(m_old, jnp.max(s, axis=0, keepdims=True))
            alpha = jnp.exp(m_old - m_new)
            pm = jnp.exp(s - m_new)
            l_new = alpha * l_sc[hd, :, cols] + jnp.sum(pm, axis=0, keepdims=True)
            acc = alpha * acc_sc[hd, :, cols] + _dot(vt_ref[0, hd, :, 0:kv_len], pm.astype(bf16))
            if diag:
                row0 = pl.multiple_of(hd * V_DIM, V_DIM)
                o_ref[0, pl.ds(row0, V_DIM), cols] = (acc / l_new).astype(o_ref.dtype)
            else:
                m_sc[hd, :, cols] = m_new
                l_sc[hd, :, cols] = l_new
                acc_sc[hd, :, cols] = acc

    @pl.when(ki != qi)
    def _():
        lax.fori_loop(0, MLA_HEADS, lambda hd, c: (head_step(hd, False), c)[1], 0)

    @pl.when(ki == qi)
    def _():
        lax.fori_loop(0, MLA_HEADS, lambda hd, c: (head_step(hd, True), c)[1], 0)


def _mla_prompt(q, k, vt):
    bv, _, sv, _ = q.shape
    tile = min(ATT_TILE, sv)
    sub = min(ATT_SUB, tile)
    nq = sv // tile
    pairs = [(i, j) for i in range(nq) for j in range(i + 1)]
    qi = jnp.asarray(np.array([a for a, _ in pairs], np.int32))
    ki = jnp.asarray(np.array([b for _, b in pairs], np.int32))
    hv = MLA_HEADS * V_DIM
    grid_spec = pltpu.PrefetchScalarGridSpec(
        num_scalar_prefetch=2,
        grid=(bv, len(pairs)),
        in_specs=[pl.BlockSpec((1, MLA_HEADS, tile, LANE), lambda b, p, qi, ki: (b, 0, qi[p], 0)),
                  pl.BlockSpec((1, MLA_HEADS, tile, LANE), lambda b, p, qi, ki: (b, 0, ki[p], 0)),
                  pl.BlockSpec((1, MLA_HEADS, V_DIM, tile), lambda b, p, qi, ki: (b, 0, 0, ki[p]))],
        out_specs=pl.BlockSpec((1, hv, tile), lambda b, p, qi, ki: (b, 0, qi[p])),
        scratch_shapes=[pltpu.VMEM((MLA_HEADS, 1, tile), f32), pltpu.VMEM((MLA_HEADS, 1, tile), f32),
                        pltpu.VMEM((MLA_HEADS, V_DIM, tile), f32)],
    )
    return pl.pallas_call(
        functools.partial(_mla_prompt_kernel, tile=tile, sub=sub),
        grid_spec=grid_spec,
        out_shape=jax.ShapeDtypeStruct((bv, hv, sv), bf16),
        compiler_params=_params(("parallel", "arbitrary"), 48),
        name="mla_prompt",
    )(qi, ki, q, k, vt)


def _mla_sample_kernel(q_ref, kp_ref, vp_ref, kn_ref, vn_ref, o_ref):
    vp = vp_ref[0]
    vn = vn_ref[0]
    lane_head = lax.broadcasted_iota(jnp.int32, (1, MLA_HEADS * V_DIM), 1) // V_DIM
    out = jnp.zeros(o_ref.shape[1:], f32)
    for hd in range(MLA_HEADS):
        qh = q_ref[0, hd]
        s1 = _dot_nt(qh, kp_ref[0, hd])
        s2 = _dot_nt(qh, kn_ref[0, hd])
        m = jnp.maximum(jnp.max(s1, axis=-1, keepdims=True), jnp.max(s2, axis=-1, keepdims=True))
        p1 = jnp.exp(s1 - m)
        p2 = jnp.exp(s2 - m)
        l = jnp.sum(p1, axis=-1, keepdims=True) + jnp.sum(p2, axis=-1, keepdims=True)
        sel = lane_head == hd
        o = _dot(p1.astype(bf16), jnp.where(sel, vp, 0)) + _dot(p2.astype(bf16), jnp.where(sel, vn, 0))
        out = out + o / l
    o_ref[0] = out.astype(o_ref.dtype)


def _mla_sample(q, kp, vp, kn, vn):
    bv, _, sv, _ = kn.shape
    past = kp.shape[2]
    hv = MLA_HEADS * V_DIM
    return pl.pallas_call(
        _mla_sample_kernel,
        grid=(bv,),
        in_specs=[pl.BlockSpec((1, MLA_HEADS, sv, LANE), lambda b: (0, 0, b, 0)),
                  pl.BlockSpec((1, MLA_HEADS, past, LANE), lambda b: (b, 0, 0, 0)),
                  pl.BlockSpec((1, past, hv), lambda b: (b, 0, 0)),
                  pl.BlockSpec((1, MLA_HEADS, sv, LANE), lambda b: (b, 0, 0, 0)),
                  pl.BlockSpec((1, sv, hv), lambda b: (b, 0, 0))],
        out_specs=pl.BlockSpec((1, sv, hv), lambda b: (b, 0, 0)),
        out_shape=jax.ShapeDtypeStruct((bv, sv, hv), bf16),
        compiler_params=_params(("parallel",), 32),
        name="mla_sample",
    )(q, kp, vp, kn, vn)


def _mlstm_kernel(mqk_ref, mv_ref, gate_ref, conv0_ref, c0_ref, n0_ref, m0_ref, wconv_ref, bconv_ref, bgate_ref,
                  h_ref, cout_ref, nout_ref, mout_ref, convout_ref, ext_sc, c_sc, n_sc, m_sc, *, chunk):
    step = pl.program_id(1)
    L = chunk

    @pl.when(step == 0)
    def _():
        ext_sc[0:SUBLANE, :] = conv0_ref[0]
        c_sc[...] = c0_ref[0]
        n_sc[...] = n0_ref[0]
        m_sc[...] = m0_ref[0]

    x = mqk_ref[0]
    ext_sc[SUBLANE:SUBLANE + L, :] = x
    y = bconv_ref[...] + x * wconv_ref[CONV_W - 1:CONV_W, :]
    for j in range(CONV_W - 1):
        off = SUBLANE - (CONV_W - 1) + j
        y = y + ext_sc[off:off + L, :] * wconv_ref[j:j + 1, :]
    ext_sc[0:SUBLANE, :] = x[L - SUBLANE:L, :]
    qk = y * _sigmoid(y)
    mq = qk[:, :ML_WIDTH]
    mk = qk[:, ML_WIDTH:] * (ML_DH ** -0.5)
    mv = mv_ref[0]

    g = gate_ref[0] + bgate_ref[...]
    ls = jnp.minimum(g, 0.0) - jnp.log1p(jnp.exp(-jnp.abs(g)))
    r_i = lax.broadcasted_iota(jnp.int32, (L, L), 0)
    c_i = lax.broadcasted_iota(jnp.int32, (L, L), 1)
    causal = c_i <= r_i
    tri = causal.astype(f32)
    bcum = jnp.dot(tri, ls, preferred_element_type=f32, precision=lax.Precision.HIGHEST)
    g_t = g.T[0:SUBLANE, :]
    bcum_t = lax.dot_general(ls.T[0:SUBLANE, :], tri, NT_DIMS, preferred_element_type=f32,
                             precision=lax.Precision.HIGHEST)

    lane = lax.broadcasted_iota(jnp.int32, (1, LANE), 1)
    sub8 = lax.broadcasted_iota(jnp.int32, (SUBLANE, 1), 0)
    last_row = lax.broadcasted_iota(jnp.int32, (L, 1), 0) == L - 1

    def col(t, idx):
        return jnp.sum(jnp.where(lane == idx, t, 0.0), axis=1, keepdims=True)

    def row(t, idx):
        return jnp.sum(jnp.where(sub8 == idx, t, 0.0), axis=0, keepdims=True)

    m_vec = m_sc[...]
    m_next = jnp.zeros_like(m_vec)
    for hd in range(ML_HEADS):
        hs = slice(hd * ML_DH, (hd + 1) * ML_DH)
        b_col, ig_col = col(bcum, GATE_F + hd), col(g, hd)
        b_row, ig_row = row(bcum_t, GATE_F + hd), row(g_t, hd)
        m_prev = col(m_vec, GATE_F + hd)
        inter = b_col + m_prev
        d = jnp.where(causal, b_col - b_row + ig_row, -jnp.inf)
        m_t = jnp.maximum(inter, jnp.max(d, axis=1, keepdims=True))
        w = jnp.exp(d - m_t)
        a_inter = jnp.exp(inter - m_t)
        qh, kh, vh = mq[:, hs], mk[:, hs], mv[:, hs]
        qb, vb = qh.astype(bf16), vh.astype(bf16)
        sqk = _dot_nt(qb, kh.astype(bf16)) * w
        c_h = c_sc[hd]
        n_h = n_sc[hd:hd + 1, :]
        num = a_inter * _dot(qb, c_h.astype(bf16)) + _dot(sqk.astype(bf16), vb)
        qn = a_inter * jnp.sum(qh * n_h, axis=1, keepdims=True) + jnp.sum(sqk, axis=1, keepdims=True)
        h_ref[0, :, hs] = num / jnp.maximum(jnp.abs(qn), jnp.exp(-m_t))
        b_last = jnp.sum(jnp.where(last_row, b_col, 0.0), axis=0, keepdims=True)
        m_end = jnp.sum(jnp.where(last_row, m_t, 0.0), axis=0, keepdims=True)
        decay = jnp.exp(b_last + m_prev - m_end)
        kw = kh * jnp.exp(b_last - b_col + ig_col - m_end)
        c_sc[hd] = decay * c_h + _dot(kw.T.astype(bf16), vb)
        n_sc[hd:hd + 1, :] = decay * n_h + jnp.sum(kw, axis=0, keepdims=True)
        m_next = m_next + jnp.where(lane == GATE_F + hd, m_end, 0.0)
    m_sc[...] = m_next

    @pl.when(step == pl.num_programs(1) - 1)
    def _():
        cout_ref[0] = c_sc[...]
        nout_ref[0] = n_sc[...]
        mout_ref[0] = m_sc[...]
        convout_ref[0] = ext_sc[0:SUBLANE, :]


def _mlstm(mqk, mv, gate, conv0, c0, n0, m0, W, chunk):
    bv, sv, _ = mqk.shape
    step = lambda w: pl.BlockSpec((1, chunk, w), lambda b, c: (b, c, 0))
    per_b = lambda *s: pl.BlockSpec((1,) + s, lambda b, c: (b,) + (0,) * len(s))
    sds = jax.ShapeDtypeStruct
    return pl.pallas_call(
        functools.partial(_mlstm_kernel, chunk=chunk),
        grid=(bv, sv // chunk),
        in_specs=[step(2 * ML_WIDTH), step(ML_WIDTH), step(LANE),
                  per_b(SUBLANE, 2 * ML_WIDTH), per_b(ML_HEADS, ML_DH, ML_DH), per_b(ML_HEADS, ML_DH), per_b(1, LANE),
                  _const_spec((CONV_W, 2 * ML_WIDTH)), _const_spec((1, 2 * ML_WIDTH)), _const_spec((1, LANE))],
        out_specs=[step(ML_WIDTH), per_b(ML_HEADS, ML_DH, ML_DH), per_b(ML_HEADS, ML_DH), per_b(1, LANE),
                   per_b(SUBLANE, 2 * ML_WIDTH)],
        out_shape=[sds((bv, sv, ML_WIDTH), f32), sds((bv, ML_HEADS, ML_DH, ML_DH), f32), sds((bv, ML_HEADS, ML_DH), f32),
                   sds((bv, 1, LANE), f32), sds((bv, SUBLANE, 2 * ML_WIDTH), f32)],
        scratch_shapes=[pltpu.VMEM((SUBLANE + chunk, 2 * ML_WIDTH), f32), pltpu.VMEM((ML_HEADS, ML_DH, ML_DH), f32),
                        pltpu.VMEM((ML_HEADS, ML_DH), f32), pltpu.VMEM((1, LANE), f32)],
        compiler_params=_params(("parallel", "arbitrary"), 32),
        name="mlstm",
    )(mqk, mv, gate, conv0, c0, n0, m0, W["w_conv"], W["b_conv"], W["b_gate"])


def _outproj_kernel(x_ref, a_ref, hm_ref, og_ref, gmh_ref, wa_ref, wm_ref, o_ref, *, a_transposed):
    if a_transposed:
        a = a_ref[0].astype(f32).T.astype(bf16)
    else:
        a = a_ref[0]
    hm = hm_ref[0]
    parts = []
    for hd in range(ML_HEADS):
        t = hm[:, hd * ML_DH:(hd + 1) * ML_DH]
        parts.append(t * lax.rsqrt(jnp.mean(t * t, axis=-1, keepdims=True) + EPS))
    hn = jnp.concatenate(parts, axis=1) * gmh_ref[...] * _sigmoid(og_ref[0])
    o_ref[0] = x_ref[0] + _dot(a, wa_ref[...]) + _dot(hn.astype(bf16), wm_ref[...])


def _outproj(x, a, hm, og, W, a_transposed):
    bv, sv, _ = x.shape
    tm = min(ROW_TILE, sv)
    hv = MLA_HEADS * V_DIM
    row = lambda w: pl.BlockSpec((1, tm, w), lambda b, j: (b, j, 0))
    a_spec = pl.BlockSpec((1, hv, tm), lambda b, j: (b, 0, j)) if a_transposed else row(hv)
    return pl.pallas_call(
        functools.partial(_outproj_kernel, a_transposed=a_transposed),
        grid=(bv, sv // tm),
        in_specs=[row(D_MODEL), a_spec, row(ML_WIDTH), row(ML_WIDTH), _const_spec((1, ML_WIDTH)),
                  _const_spec((hv, D_MODEL)), _const_spec((ML_WIDTH, D_MODEL))],
        out_specs=row(D_MODEL),
        out_shape=jax.ShapeDtypeStruct(x.shape, f32),
        compiler_params=_params(("parallel", "parallel"), 32),
        name="outproj_t" if a_transposed else "outproj",
    )(x, a, hm, og, W["g_mh"], W["w_out_a"], W["w_out_m"])


def _xattn_kernel(x_ref, mk_ref, mv_ref, gx_ref, wq_ref, gxq_ref, wo_ref, o_ref):
    x = x_ref[0]
    qx = _dot(_rms(x, gx_ref[...]).astype(bf16), wq_ref[...])
    mk = mk_ref[0].astype(bf16)
    mv = mv_ref[0].astype(bf16)
    gxq = gxq_ref[...]
    outs = []
    for hd in range(XA_HEADS):
        hs = slice(hd * XA_DH, (hd + 1) * XA_DH)
        t = _rms(qx[:, hs], gxq) * XA_SCALE
        s = _dot_nt(t.astype(bf16), mk[:, hs])
        pm = jnp.exp(s - jnp.max(s, axis=-1, keepdims=True))
        outs.append(_dot(pm.astype(bf16), mv[:, hs]) / jnp.sum(pm, axis=-1, keepdims=True))
    ox = jnp.concatenate(outs, axis=1)
    o_ref[0] = x + _dot(ox.astype(bf16), wo_ref[...])


def _xattn(x, mem_k, mem_v, W):
    bv, sv, _ = x.shape
    tm = min(ROW_TILE, sv)
    n_mem = mem_k.shape[1]
    row = pl.BlockSpec((1, tm, D_MODEL), lambda b, j: (b, j, 0))
    mem = pl.BlockSpec((1, n_mem, D_MODEL), lambda b, j: (b, 0, 0))
    return pl.pallas_call(
        _xattn_kernel,
        grid=(bv, sv // tm),
        in_specs=[row, mem, mem, _const_spec((1, D_MODEL)), _const_spec((D_MODEL, D_MODEL)),
                  _const_spec((1, XA_DH)), _const_spec((D_MODEL, D_MODEL))],
        out_specs=row,
        out_shape=jax.ShapeDtypeStruct(x.shape, f32),
        compiler_params=_params(("parallel", "parallel"), 40),
        name="xattn",
    )(x, mem_k, mem_v, W["g_xattn"], W["w_xq"], W["g_xq"], W["w_xo"])


def _mlp_kernel(x_ref, g_ref, w1_ref, w2_ref, o_ref):
    x = x_ref[0]
    hf = _rms(x, g_ref[...]).astype(bf16)
    acc = x
    for c in range(D_FF // D_MODEL):
        cs = slice(c * D_MODEL, (c + 1) * D_MODEL)
        u = jnp.square(jnp.maximum(_dot(hf, w1_ref[:, cs]), 0.0))
        acc = acc + _dot(u.astype(bf16), w2_ref[cs, :])
    o_ref[0] = acc


def _mlp(x, W):
    bv, sv, _ = x.shape
    tm = min(ROW_TILE, sv)
    row = pl.BlockSpec((1, tm, D_MODEL), lambda b, j: (b, j, 0))
    return pl.pallas_call(
        _mlp_kernel,
        grid=(bv, sv // tm),
        in_specs=[row, _const_spec((1, D_MODEL)), _const_spec((D_MODEL, D_FF)), _const_spec((D_FF, D_MODEL))],
        out_specs=row,
        out_shape=jax.ShapeDtypeStruct(x.shape, f32),
        compiler_params=_params(("parallel", "parallel"), 48),
        name="mlp",
    )(x, W["g_mlp"], W["w_ff1"], W["w_ff2"])


def _memkv_kernel(mem_ref, gm_ref, wk_ref, wv_ref, gk_ref, k_ref, v_ref):
    hm = _rms(mem_ref[0], gm_ref[...]).astype(bf16)
    k = _dot(hm, wk_ref[...])
    gk = gk_ref[...]
    k_ref[0] = jnp.concatenate([_rms(k[:, hd * XA_DH:(hd + 1) * XA_DH], gk) for hd in range(XA_HEADS)], axis=1)
    v_ref[0] = _dot(hm, wv_ref[...])


def _memkv(mem, W):
    bv, n_mem, _ = mem.shape
    blk = pl.BlockSpec((1, n_mem, D_MODEL), lambda b: (b, 0, 0))
    return pl.pallas_call(
        _memkv_kernel,
        grid=(bv,),
        in_specs=[blk, _const_spec((1, D_MODEL)), _const_spec((D_MODEL, D_MODEL)), _const_spec((D_MODEL, D_MODEL)),
                  _const_spec((1, XA_DH))],
        out_specs=[blk, blk],
        out_shape=[jax.ShapeDtypeStruct(mem.shape, f32)] * 2,
        compiler_params=_params(("parallel",), 32),
        name="memkv",
    )(mem, W["g_mem"], W["w_xk"], W["w_xv"], W["g_xk"])


def _pad_lanes(a, lo, width=LANE):
    pad = [(0, 0)] * (a.ndim - 1) + [(lo, width - lo - a.shape[-1])]
    return jnp.pad(a, pad)


def _rope_tables(pos):
    half = ROPE_DIM // 2
    inv_freq = ROPE_THETA ** (-jnp.arange(half, dtype=f32) / half)
    ang = pos.astype(f32)[:, None] * inv_freq[None, :]
    cos, sin = jnp.cos(ang), jnp.sin(ang)
    n = pos.shape[0]
    ones, zeros = jnp.ones((n, NOPE_DIM), f32), jnp.zeros((n, half), f32)
    tail = jnp.zeros((n, LANE - QK_DIM), f32)
    cos_t = jnp.concatenate([ones, cos, cos, tail], axis=1)
    sin_a = jnp.concatenate([0 * ones, -sin, zeros, tail], axis=1)
    sin_b = jnp.concatenate([0 * ones, zeros, sin, tail], axis=1)
    return cos_t, sin_a, sin_b


def _layer_weights(l, g_mix, w_in, g_qa, w_q_up, g_qnorm, g_kva, w_kv_up, g_knorm, w_conv, b_conv, b_igate, b_fgate,
                   g_mhead, w_out, g_xattn, g_mem, w_xq, w_xk, w_xv, g_xq, g_xk, w_xo, g_mlp, w_ff1, w_ff2):
    wi = w_in[l]
    off_kr = Q_RANK + KV_RANK
    off_mqk = off_kr + ROPE_DIM
    off_mv = off_mqk + 2 * ML_WIDTH
    off_mi = off_mv + ML_WIDTH
    off_mo = off_mi + 2 * ML_HEADS
    packed = jnp.concatenate([
        wi[:, :off_kr],
        _pad_lanes(wi[:, off_kr:off_mqk], NOPE_DIM),
        wi[:, off_mqk:off_mi],
        wi[:, off_mo:],
        _pad_lanes(wi[:, off_mi:off_mo], 0),
    ], axis=1)
    wq = w_q_up[l].reshape(Q_RANK, MLA_HEADS, QK_DIM)
    wkv = w_kv_up[l].reshape(KV_RANK, MLA_HEADS, NOPE_DIM + V_DIM)
    wv = wkv[:, :, NOPE_DIM:].reshape(KV_RANK, MLA_HEADS * V_DIM)
    row = lambda a: a.reshape(1, -1)
    return {
        "g_mix": row(g_mix[l]), "w_in": packed.astype(bf16), "g_qa": row(g_qa[l]),
        "w_q": _pad_lanes(wq, 0).reshape(Q_RANK, MLA_HEADS * LANE).astype(bf16),
        "g_qn": _pad_lanes(row(g_qnorm[l]), 0), "g_kva": row(g_kva[l]),
        "w_k": _pad_lanes(wkv[:, :, :NOPE_DIM], 0).reshape(KV_RANK, MLA_HEADS * LANE).astype(bf16),
        "w_v": wv.astype(bf16), "w_vT": wv.T.astype(bf16),
        "g_kn": _pad_lanes(row(g_knorm[l]), 0),
        "w_conv": w_conv[l], "b_conv": row(b_conv[l]),
        "b_gate": _pad_lanes(row(jnp.concatenate([b_igate[l], b_fgate[l]])), 0),
        "g_mh": row(g_mhead[l]),
        "w_out_a": w_out[l][:MLA_HEADS * V_DIM].astype(bf16), "w_out_m": w_out[l][MLA_HEADS * V_DIM:].astype(bf16),
        "g_xattn": row(g_xattn[l]), "g_mem": row(g_mem[l]),
        "w_xq": w_xq[l].astype(bf16), "w_xk": w_xk[l].astype(bf16), "w_xv": w_xv[l].astype(bf16),
        "g_xq": row(g_xq[l]), "g_xk": row(g_xk[l]), "w_xo": w_xo[l].astype(bf16),
        "g_mlp": row(g_mlp[l]), "w_ff1": w_ff1[l].astype(bf16), "w_ff2": w_ff2[l].astype(bf16),
    }


def _unpack_state(c_new, n_new, m_new, conv_new):
    return (c_new, n_new, m_new[:, 0, GATE_F:GATE_F + ML_HEADS], conv_new[:, SUBLANE - (CONV_W - 1):, :])


def kernel(x_prompt, x_sample, cache_mla_ckv, cache_mla_krope, state_mlstm_C, state_mlstm_n, state_mlstm_m,
           state_mlstm_conv, cache_mem_k, cache_mem_v, mem_prompt, g_mix, w_in, g_qa, w_q_up, g_qnorm, g_kva,
           w_kv_up, g_knorm, w_conv, b_conv, b_igate, b_fgate, g_mhead, w_out, g_xattn, g_mem, w_xq, w_xk, w_xv,
           g_xq, g_xk, w_xo, g_mlp, w_ff1, w_ff2):
    depth = w_in.shape[0]
    bp, sp, _ = x_prompt.shape
    bs, ss, _ = x_sample.shape
    past = cache_mla_ckv.shape[2]
    n_mem = mem_prompt.shape[1]
    weights = (g_mix, w_in, g_qa, w_q_up, g_qnorm, g_kva, w_kv_up, g_knorm, w_conv, b_conv, b_igate, b_fgate,
               g_mhead, w_out, g_xattn, g_mem, w_xq, w_xk, w_xv, g_xq, g_xk, w_xo, g_mlp, w_ff1, w_ff2)

    tabs_p = _rope_tables(jnp.arange(sp))
    tabs_s = tuple(jnp.tile(t, (bs, 1)) for t in _rope_tables(past + jnp.arange(ss)))
    ml_chunk = min(ML_CHUNK, sp)

    xp, xs = x_prompt, x_sample.reshape(1, bs * ss, D_MODEL)
    p_out = [[] for _ in range(8)]
    s_out = [[] for _ in range(6)]
    for l in range(depth):
        W = _layer_weights(l, *weights)

        mem_k, mem_v = _memkv(mem_prompt, W)
        q, ckv, kr, mqk, mv, og, gate = _inproj(xp, tabs_p, W)
        k, vt = _kvup(ckv, kr, W, True)
        a = _mla_prompt(q, k, vt.reshape(bp, MLA_HEADS, V_DIM, sp))
        zeros = lambda *s: jnp.zeros((bp,) + s, f32)
        hm, *state = _mlstm(mqk, mv, gate, zeros(SUBLANE, 2 * ML_WIDTH), zeros(ML_HEADS, ML_DH, ML_DH),
                            zeros(ML_HEADS, ML_DH), zeros(1, LANE), W, ml_chunk)
        xp = _outproj(xp, a, hm, og, W, True)
        xp = _xattn(xp, mem_k, mem_v, W)
        xp = _mlp(xp, W)
        new = (ckv, kr[..., NOPE_DIM:QK_DIM]) + _unpack_state(*state) + (
            mem_k.reshape(bp, n_mem, XA_HEADS, XA_DH), mem_v.reshape(bp, n_mem, XA_HEADS, XA_DH))
        for lst, t in zip(p_out, new):
            lst.append(t)

        q, ckv, kr, mqk, mv, og, gate = _inproj(xs, tabs_s, W)
        per_stream = lambda t: t.reshape(bs, ss, t.shape[-1])
        ckv, kr = per_stream(ckv), per_stream(kr)
        kn, vn = _kvup(ckv, kr, W, False)
        kp, vp = _kvup(cache_mla_ckv[l], _pad_lanes(cache_mla_krope[l], NOPE_DIM), W, False)
        a = _mla_sample(q, kp, vp, kn, vn)
        conv0 = jnp.pad(state_mlstm_conv[l], ((0, 0), (SUBLANE - (CONV_W - 1), 0), (0, 0)))
        m0 = _pad_lanes(state_mlstm_m[l], GATE_F).reshape(bs, 1, LANE)
        hm, *state = _mlstm(per_stream(mqk), per_stream(mv), per_stream(gate), conv0, state_mlstm_C[l],
                            state_mlstm_n[l], m0, W, ss)
        flat = lambda t: t.reshape(1, bs * ss, t.shape[-1])
        xs = _outproj(xs, flat(a), flat(hm), og, W, False)
        xs = flat(_xattn(per_stream(xs), cache_mem_k[l].reshape(bs, n_mem, D_MODEL),
                         cache_mem_v[l].reshape(bs, n_mem, D_MODEL), W))
        xs = _mlp(xs, W)
        new = (ckv, kr[..., NOPE_DIM:QK_DIM]) + _unpack_state(*state)
        for lst, t in zip(s_out, new):
            lst.append(t)

    outs_p = tuple(jnp.stack(t) for t in p_out)
    outs_s = tuple(jnp.stack(t) for t in s_out)
    return (xp, xs.reshape(bs, ss, D_MODEL)) + outs_p + outs_s
```

```python
import functools

import jax
import jax.numpy as jnp
import numpy as np
from jax import lax
from jax.experimental import pallas as pl
from jax.experimental.pallas import tpu as pltpu

f32 = jnp.float32
bf16 = jnp.bfloat16

D_MODEL = 1024
CHUNK = 64
EPS = 1e-6
MLA_HEADS = 8
Q_RANK = 256
KV_RANK = 128
NOPE_DIM = 64
ROPE_DIM = 32
QK_DIM = NOPE_DIM + ROPE_DIM
V_DIM = 64
ROPE_THETA = 10000.0
MLA_SCALE = QK_DIM ** -0.5
ML_HEADS = 4
ML_DH = 128
ML_WIDTH = ML_HEADS * ML_DH
CONV_W = 4
XA_HEADS = 4
XA_DH = D_MODEL // XA_HEADS
XA_SCALE = XA_DH ** -0.5
D_FF = 4 * D_MODEL

LANE = 128
SUBLANE = 8
MIB = 1024 * 1024

C_QA = 0
C_KVA = C_QA + Q_RANK
C_KR = C_KVA + KV_RANK
C_MQK = C_KR + LANE
C_MV = C_MQK + 2 * ML_WIDTH
C_MO = C_MV + ML_WIDTH
C_GATE = C_MO + ML_WIDTH
IN_COLS_PACKED = C_GATE + LANE
GATE_F = ML_HEADS

ROW_TILE = 512
ATT_TILE = 1024
ATT_SUB = 1024
ATT_KB = 1024
ML_CHUNK = 256

BF16_ROWS = 16
V_EXT = V_DIM + BF16_ROWS
LOG2E = 1.4426950408889634
SAFE_LOG2_SCORE = 64.0

NT_DIMS = (((1,), (1,)), ((), ()))


def _rms(x, g):
    return x * lax.rsqrt(jnp.mean(x * x, axis=-1, keepdims=True) + EPS) * g


def _dot(a, b):
    return jnp.dot(a, b, preferred_element_type=f32)


def _dot_nt(a, b):
    return lax.dot_general(a, b, NT_DIMS, preferred_element_type=f32)


def _sigmoid(x):
    return 1.0 / (1.0 + jnp.exp(-x))


def _const_spec(shape):
    nd = len(shape)
    return pl.BlockSpec(shape, lambda *_: (0,) * nd, pipeline_mode=pl.Buffered(1))


def _params(semantics, vmem_mib):
    return pltpu.CompilerParams(dimension_semantics=semantics, vmem_limit_bytes=vmem_mib * MIB)


def _inproj_kernel(x_ref, cos_ref, sa_ref, sb_ref, gmix_ref, win_ref, gqa_ref, wq_ref, gqn_ref, gkva_ref,
                   q_ref, ckv_ref, kr_ref, mqk_ref, mv_ref, og_ref, gate_ref, qn_ref):
    x = x_ref[0]
    h = _rms(x, gmix_ref[...]).astype(bf16)
    proj = lambda lo, hi: _dot(h, win_ref[:, lo:hi])
    cos, sa, sb = cos_ref[...], sa_ref[...], sb_ref[...]

    def rope(t):
        return t * cos + pltpu.roll(t, LANE - ROPE_DIM // 2, 1) * sa + pltpu.roll(t, ROPE_DIM // 2, 1) * sb

    q_lat = _rms(proj(C_QA, C_KVA), gqa_ref[...])
    qf = _dot(q_lat.astype(bf16), wq_ref[...])
    gq = gqn_ref[...]
    norm2 = None
    for hd in range(MLA_HEADS):
        t = rope(qf[:, hd * LANE:(hd + 1) * LANE])
        ss = jnp.sum(t * t, axis=-1, keepdims=True) * (1.0 / QK_DIM)
        qh = t * lax.rsqrt(ss + EPS) * gq * (MLA_SCALE * LOG2E)
        q_ref[0, hd] = qh.astype(bf16)
        n2 = jnp.sum(qh * qh, axis=-1, keepdims=True)
        norm2 = n2 if norm2 is None else jnp.maximum(norm2, n2)
    qn_ref[0, 0] = jnp.broadcast_to(jnp.max(norm2, axis=0, keepdims=True), qn_ref.shape[2:])
    latent = proj(C_KVA, C_MQK)
    ckv_ref[0] = _rms(latent[:, :KV_RANK], gkva_ref[...])
    kr_ref[0] = rope(latent[:, KV_RANK:])
    mqk_ref[0] = proj(C_MQK, C_MV)
    mv_ref[0] = proj(C_MV, C_MO)
    og_ref[0] = proj(C_MO, C_GATE)
    gate_ref[0] = proj(C_GATE, IN_COLS_PACKED)


def _inproj(x, tabs, W):
    bv, sv, _ = x.shape
    tm = min(ROW_TILE, sv)
    row = lambda w: pl.BlockSpec((1, tm, w), lambda b, j: (b, j, 0))
    tab = pl.BlockSpec((tm, LANE), lambda b, j: (j, 0))
    sds = jax.ShapeDtypeStruct
    return pl.pallas_call(
        _inproj_kernel,
        grid=(bv, sv // tm),
        in_specs=[row(D_MODEL), tab, tab, tab,
                  _const_spec((1, D_MODEL)), _const_spec((D_MODEL, IN_COLS_PACKED)),
                  _const_spec((1, Q_RANK)), _const_spec((Q_RANK, MLA_HEADS * LANE)),
                  _const_spec((1, LANE)), _const_spec((1, KV_RANK))],
        out_specs=[pl.BlockSpec((1, MLA_HEADS, tm, LANE), lambda b, j: (b, 0, j, 0)),
                   row(KV_RANK), row(LANE), row(2 * ML_WIDTH), row(ML_WIDTH), row(ML_WIDTH), row(LANE),
                   pl.BlockSpec((1, 1, SUBLANE, LANE), lambda b, j: (b, j, 0, 0))],
        out_shape=[sds((bv, MLA_HEADS, sv, LANE), bf16), sds((bv, sv, KV_RANK), f32), sds((bv, sv, LANE), f32),
                   sds((bv, sv, 2 * ML_WIDTH), f32), sds((bv, sv, ML_WIDTH), f32), sds((bv, sv, ML_WIDTH), f32),
                   sds((bv, sv, LANE), f32), sds((bv, sv // tm, SUBLANE, LANE), f32)],
        compiler_params=_params(("parallel", "parallel"), 48),
        name="inproj",
    )(x, *tabs, W["g_mix"], W["w_in"], W["g_qa"], W["w_q"], W["g_qn"], W["g_kva"])


def _kvup_kernel(ckv_ref, kr_ref, wk_ref, wv_ref, gk_ref, k_ref, v_ref, *maybe_kn_ref, transposed_v):
    c = ckv_ref[0].astype(bf16)
    kn = _dot(c, wk_ref[...])
    kr = kr_ref[0]
    g = gk_ref[...]
    norm2 = None
    for hd in range(MLA_HEADS):
        t = kn[:, hd * LANE:(hd + 1) * LANE] + kr
        ss = jnp.sum(t * t, axis=-1, keepdims=True) * (1.0 / QK_DIM)
        kh = t * lax.rsqrt(ss + EPS) * g
        k_ref[0, hd] = kh.astype(bf16)
        n2 = jnp.sum(kh * kh, axis=-1, keepdims=True)
        norm2 = n2 if norm2 is None else jnp.maximum(norm2, n2)
    if transposed_v:
        vt = _dot_nt(wv_ref[...], c).astype(bf16)
        tm = vt.shape[1]
        ones_tile = (lax.broadcasted_iota(jnp.int32, (BF16_ROWS, tm), 0) == 0).astype(bf16)
        for hd in range(MLA_HEADS):
            v_ref[0, hd, 0:V_DIM, :] = vt[hd * V_DIM:(hd + 1) * V_DIM, :]
            v_ref[0, hd, V_DIM:V_EXT, :] = ones_tile
        kn_ref, = maybe_kn_ref
        kn_ref[0, 0] = jnp.broadcast_to(jnp.max(norm2, axis=0, keepdims=True), kn_ref.shape[2:])
    else:
        v_ref[0] = _dot(c, wv_ref[...]).astype(bf16)


def _kvup(ckv, kr, W, transposed_v):
    bv, sv, _ = ckv.shape
    tm = min(ROW_TILE, sv)
    hv = MLA_HEADS * V_DIM
    row = lambda w: pl.BlockSpec((1, tm, w), lambda b, j: (b, j, 0))
    out_specs = [pl.BlockSpec((1, MLA_HEADS, tm, LANE), lambda b, j: (b, 0, j, 0))]
    out_shape = [jax.ShapeDtypeStruct((bv, MLA_HEADS, sv, LANE), bf16)]
    if transposed_v:
        out_specs += [pl.BlockSpec((1, MLA_HEADS, V_EXT, tm), lambda b, j: (b, 0, 0, j)),
                      pl.BlockSpec((1, 1, SUBLANE, LANE), lambda b, j: (b, j, 0, 0))]
        out_shape += [jax.ShapeDtypeStruct((bv, MLA_HEADS, V_EXT, sv), bf16),
                      jax.ShapeDtypeStruct((bv, sv // tm, SUBLANE, LANE), f32)]
        wv = W["w_vT"]
    else:
        out_specs += [row(hv)]
        out_shape += [jax.ShapeDtypeStruct((bv, sv, hv), bf16)]
        wv = W["w_v"]
    return pl.pallas_call(
        functools.partial(_kvup_kernel, transposed_v=transposed_v),
        grid=(bv, sv // tm),
        in_specs=[row(KV_RANK), row(LANE), _const_spec((KV_RANK, MLA_HEADS * LANE)), _const_spec(wv.shape),
                  _const_spec((1, LANE))],
        out_specs=out_specs,
        out_shape=out_shape,
        compiler_params=_params(("parallel", "parallel"), 32),
        name="kvup_t" if transposed_v else "kvup",
    )(ckv, kr, W["w_k"], wv, W["g_kn"])


def _mla_prompt_kernel(qi_ref, ki_ref, slow_ref, q_ref, k_ref, vt_ref, o_ref, m_sc, acc_sc, *, tile, sub, kb):
    p = pl.program_id(1)
    qi, ki = qi_ref[p], ki_ref[p]
    slow = slow_ref[0] != 0
    nsub = tile // sub
    kd = min(kb, sub)
    key_chunk = lax.broadcasted_iota(jnp.int32, (kd, 1), 0) // CHUNK
    query_chunk = lax.broadcasted_iota(jnp.int32, (1, sub), 1) // CHUNK

    @pl.when(ki == 0)
    def _():
        m_sc[...] = jnp.full(m_sc.shape, -jnp.inf, f32)
        acc_sc[...] = jnp.zeros(acc_sc.shape, f32)

    def scores(hd, lo, hi, cols):
        return _dot_nt(k_ref[0, hd, lo:hi, :], q_ref[0, hd, cols, :])

    def pv(hd, lo, hi, pm):
        return _dot(vt_ref[0, hd, :, lo:hi], pm.astype(bf16))

    def finish(hd, cols, acc):
        row0 = pl.multiple_of(hd * V_DIM, V_DIM)
        o_ref[0, pl.ds(row0, V_DIM), cols] = (acc[:V_DIM] / acc[V_DIM:V_DIM + 1]).astype(o_ref.dtype)

    def fast_step(hd, diag):
        for qs in range(nsub):
            cols = slice(qs * sub, (qs + 1) * sub)
            full = qs * sub if diag else tile
            acc = acc_sc[hd, :, cols]
            for lo in range(0, full, min(kb, max(full, 1))):
                hi = lo + min(kb, full)
                acc = acc + pv(hd, lo, hi, jnp.exp2(scores(hd, lo, hi, cols)))
            if diag:
                for off in range(0, sub, kd):
                    lo = full + off
                    visible = key_chunk + off // CHUNK <= query_chunk
                    pm = jnp.where(visible, jnp.exp2(scores(hd, lo, lo + kd, cols)), 0.0)
                    acc = acc + pv(hd, lo, lo + kd, pm)
                finish(hd, cols, acc)
            else:
                acc_sc[hd, :, cols] = acc

    def slow_step(hd, diag):
        for qs in range(nsub):
            cols = slice(qs * sub, (qs + 1) * sub)
            kv_len = (qs + 1) * sub if diag else tile
            s = scores(hd, 0, kv_len, cols)
            if diag:
                kc = lax.broadcasted_iota(jnp.int32, (kv_len, 1), 0) // CHUNK
                s = jnp.where(kc <= query_chunk + (qs * sub) // CHUNK, s, -jnp.inf)
            m_old = m_sc[hd, :, cols]
            m_new = jnp.maximum(m_old, jnp.max(s, axis=0, keepdims=True))
            acc = jnp.exp2(m_old - m_new) * acc_sc[hd, :, cols] + pv(hd, 0, kv_len, jnp.exp2(s - m_new))
            if diag:
                finish(hd, cols, acc)
            else:
                m_sc[hd, :, cols] = m_new
                acc_sc[hd, :, cols] = acc

    for step_fn, use in ((fast_step, jnp.logical_not(slow)), (slow_step, slow)):
        for diag, where in ((False, ki != qi), (True, ki == qi)):
            @pl.when(jnp.logical_and(use, where))
            def _(step_fn=step_fn, diag=diag):
                lax.fori_loop(0, MLA_HEADS, lambda hd, c: (step_fn(hd, diag), c)[1], 0, unroll=2)


def _mla_prompt(q, k, vt, slow):
    bv, _, sv, _ = q.shape
    tile = min(ATT_TILE, sv)
    sub = min(ATT_SUB, tile)
    nq = sv // tile
    pairs = [(i, j) for i in range(nq) for j in range(i + 1)]
    qi = jnp.asarray(np.array([a for a, _ in pairs], np.int32))
    ki = jnp.asarray(np.array([b for _, b in pairs], np.int32))
    hv = MLA_HEADS * V_DIM
    grid_spec = pltpu.PrefetchScalarGridSpec(
        num_scalar_prefetch=3,
        grid=(bv, len(pairs)),
        in_specs=[pl.BlockSpec((1, MLA_HEADS, tile, LANE), lambda b, p, qi, ki, sl: (b, 0, qi[p], 0)),
                  pl.BlockSpec((1, MLA_HEADS, tile, LANE), lambda b, p, qi, ki, sl: (b, 0, ki[p], 0)),
                  pl.BlockSpec((1, MLA_HEADS, V_EXT, tile), lambda b, p, qi, ki, sl: (b, 0, 0, ki[p]))],
        out_specs=pl.BlockSpec((1, hv, tile), lambda b, p, qi, ki, sl: (b, 0, qi[p])),
        scratch_shapes=[pltpu.VMEM((MLA_HEADS, 1, tile), f32), pltpu.VMEM((MLA_HEADS, V_EXT, tile), f32)],
    )
    return pl.pallas_call(
        functools.partial(_mla_prompt_kernel, tile=tile, sub=sub, kb=min(ATT_KB, tile)),
        grid_spec=grid_spec,
        out_shape=jax.ShapeDtypeStruct((bv, hv, sv), bf16),
        compiler_params=_params(("parallel", "arbitrary"), 48),
        name="mla_prompt",
    )(qi, ki, slow, q, k, vt)


def _mla_sample_kernel(q_ref, kp_ref, vp_ref, kn_ref, vn_ref, o_ref):
    vp = vp_ref[0]
    vn = vn_ref[0]
    lane_head = lax.broadcasted_iota(jnp.int32, (1, MLA_HEADS * V_DIM), 1) // V_DIM
    out = jnp.zeros(o_ref.shape[1:], f32)
    for hd in range(MLA_HEADS):
        qh = q_ref[0, hd]
        s1 = _dot_nt(qh, kp_ref[0, hd])
        s2 = _dot_nt(qh, kn_ref[0, hd])
        m = jnp.maximum(jnp.max(s1, axis=-1, keepdims=True), jnp.max(s2, axis=-1, keepdims=True))
        p1 = jnp.exp2(s1 - m)
        p2 = jnp.exp2(s2 - m)
        l = jnp.sum(p1, axis=-1, keepdims=True) + jnp.sum(p2, axis=-1, keepdims=True)
        sel = lane_head == hd
        o = _dot(p1.astype(bf16), jnp.where(sel, vp, 0)) + _dot(p2.astype(bf16), jnp.where(sel, vn, 0))
        out = out + o / l
    o_ref[0] = out.astype(o_ref.dtype)


def _mla_sample(q, kp, vp, kn, vn):
    bv, _, sv, _ = kn.shape
    past = kp.shape[2]
    hv = MLA_HEADS * V_DIM
    return pl.pallas_call(
        _mla_sample_kernel,
        grid=(bv,),
        in_specs=[pl.BlockSpec((1, MLA_HEADS, sv, LANE), lambda b: (0, 0, b, 0)),
                  pl.BlockSpec((1, MLA_HEADS, past, LANE), lambda b: (b, 0, 0, 0)),
                  pl.BlockSpec((1, past, hv), lambda b: (b, 0, 0)),
                  pl.BlockSpec((1, MLA_HEADS, sv, LANE), lambda b: (b, 0, 0, 0)),
                  pl.BlockSpec((1, sv, hv), lambda b: (b, 0, 0))],
        out_specs=pl.BlockSpec((1, sv, hv), lambda b: (b, 0, 0)),
        out_shape=jax.ShapeDtypeStruct((bv, sv, hv), bf16),
        compiler_params=_params(("parallel",), 32),
        name="mla_sample",
    )(q, kp, vp, kn, vn)


def _mlstm_kernel(mqk_ref, mv_ref, gate_ref, conv0_ref, c0_ref, n0_ref, m0_ref, wconv_ref, bconv_ref, bgate_ref,
                  h_ref, cout_ref, nout_ref, mout_ref, convout_ref, ext_sc, c_sc, n_sc, m_sc, *, chunk):
    step = pl.program_id(1)
    L = chunk

    @pl.when(step == 0)
    def _():
        ext_sc[0:SUBLANE, :] = conv0_ref[0]
        c_sc[...] = c0_ref[0]
        n_sc[...] = n0_ref[0]
        m_sc[...] = m0_ref[0]

    x = mqk_ref[0]
    ext_sc[SUBLANE:SUBLANE + L, :] = x
    y = bconv_ref[...] + x * wconv_ref[CONV_W - 1:CONV_W, :]
    for j in range(CONV_W - 1):
        off = SUBLANE - (CONV_W - 1) + j
        y = y + ext_sc[off:off + L, :] * wconv_ref[j:j + 1, :]
    ext_sc[0:SUBLANE, :] = x[L - SUBLANE:L, :]
    qk = y * _sigmoid(y)
    mq = qk[:, :ML_WIDTH]
    mk = qk[:, ML_WIDTH:] * (ML_DH ** -0.5)
    mv = mv_ref[0]

    g = gate_ref[0] + bgate_ref[...]
    ls = jnp.minimum(g, 0.0) - jnp.log1p(jnp.exp(-jnp.abs(g)))
    r_i = lax.broadcasted_iota(jnp.int32, (L, L), 0)
    c_i = lax.broadcasted_iota(jnp.int32, (L, L), 1)
    causal = c_i <= r_i
    tri = causal.astype(f32)
    bcum = jnp.dot(tri, ls, preferred_element_type=f32, precision=lax.Precision.HIGHEST)
    g_t = g.T[0:SUBLANE, :]
    bcum_t = lax.dot_general(ls.T[0:SUBLANE, :], tri, NT_DIMS, preferred_element_type=f32,
                             precision=lax.Precision.HIGHEST)

    lane = lax.broadcasted_iota(jnp.int32, (1, LANE), 1)
    sub8 = lax.broadcasted_iota(jnp.int32, (SUBLANE, 1), 0)
    last_row = lax.broadcasted_iota(jnp.int32, (L, 1), 0) == L - 1

    def col(t, idx):
        return jnp.sum(jnp.where(lane == idx, t, 0.0), axis=1, keepdims=True)

    def row(t, idx):
        return jnp.sum(jnp.where(sub8 == idx, t, 0.0), axis=0, keepdims=True)

    m_vec = m_sc[...]
    m_next = jnp.zeros_like(m_vec)
    for hd in range(ML_HEADS):
        hs = slice(hd * ML_DH, (hd + 1) * ML_DH)
        b_col, ig_col = col(bcum, GATE_F + hd), col(g, hd)
        b_row, ig_row = row(bcum_t, GATE_F + hd), row(g_t, hd)
        m_prev = col(m_vec, GATE_F + hd)
        inter = b_col + m_prev
        d = jnp.where(causal, b_col - b_row + ig_row, -jnp.inf)
        m_t = jnp.maximum(inter, jnp.max(d, axis=1, keepdims=True))
        w = jnp.exp(d - m_t)
        a_inter = jnp.exp(inter - m_t)
        qh, kh, vh = mq[:, hs], mk[:, hs], mv[:, hs]
        qb, vb = qh.astype(bf16), vh.astype(bf16)
        sqk = _dot_nt(qb, kh.astype(bf16)) * w
        c_h = c_sc[hd]
        n_h = n_sc[hd:hd + 1, :]
        num = a_inter * _dot(qb, c_h.astype(bf16)) + _dot(sqk.astype(bf16), vb)
        qn = a_inter * jnp.sum(qh * n_h, axis=1, keepdims=True) + jnp.sum(sqk, axis=1, keepdims=True)
        h_ref[0, :, hs] = num / jnp.maximum(jnp.abs(qn), jnp.exp(-m_t))
        b_last = jnp.sum(jnp.where(last_row, b_col, 0.0), axis=0, keepdims=True)
        m_end = jnp.sum(jnp.where(last_row, m_t, 0.0), axis=0, keepdims=True)
        decay = jnp.exp(b_last + m_prev - m_end)
        kw = kh * jnp.exp(b_last - b_col + ig_col - m_end)
        c_sc[hd] = decay * c_h + _dot(kw.T.astype(bf16), vb)
        n_sc[hd:hd + 1, :] = decay * n_h + jnp.sum(kw, axis=0, keepdims=True)
        m_next = m_next + jnp.where(lane == GATE_F + hd, m_end, 0.0)
    m_sc[...] = m_next

    @pl.when(step == pl.num_programs(1) - 1)
    def _():
        cout_ref[0] = c_sc[...]
        nout_ref[0] = n_sc[...]
        mout_ref[0] = m_sc[...]
        convout_ref[0] = ext_sc[0:SUBLANE, :]


def _mlstm(mqk, mv, gate, conv0, c0, n0, m0, W, chunk):
    bv, sv, _ = mqk.shape
    step = lambda w: pl.BlockSpec((1, chunk, w), lambda b, c: (b, c, 0))
    per_b = lambda *s: pl.BlockSpec((1,) + s, lambda b, c: (b,) + (0,) * len(s))
    sds = jax.ShapeDtypeStruct
    return pl.pallas_call(
        functools.partial(_mlstm_kernel, chunk=chunk),
        grid=(bv, sv // chunk),
        in_specs=[step(2 * ML_WIDTH), step(ML_WIDTH), step(LANE),
                  per_b(SUBLANE, 2 * ML_WIDTH), per_b(ML_HEADS, ML_DH, ML_DH), per_b(ML_HEADS, ML_DH), per_b(1, LANE),
                  _const_spec((CONV_W, 2 * ML_WIDTH)), _const_spec((1, 2 * ML_WIDTH)), _const_spec((1, LANE))],
        out_specs=[step(ML_WIDTH), per_b(ML_HEADS, ML_DH, ML_DH), per_b(ML_HEADS, ML_DH), per_b(1, LANE),
                   per_b(SUBLANE, 2 * ML_WIDTH)],
        out_shape=[sds((bv, sv, ML_WIDTH), f32), sds((bv, ML_HEADS, ML_DH, ML_DH), f32), sds((bv, ML_HEADS, ML_DH), f32),
                   sds((bv, 1, LANE), f32), sds((bv, SUBLANE, 2 * ML_WIDTH), f32)],
        scratch_shapes=[pltpu.VMEM((SUBLANE + chunk, 2 * ML_WIDTH), f32), pltpu.VMEM((ML_HEADS, ML_DH, ML_DH), f32),
                        pltpu.VMEM((ML_HEADS, ML_DH), f32), pltpu.VMEM((1, LANE), f32)],
        compiler_params=_params(("parallel", "arbitrary"), 32),
        name="mlstm",
    )(mqk, mv, gate, conv0, c0, n0, m0, W["w_conv"], W["b_conv"], W["b_gate"])


def _outproj_kernel(x_ref, a_ref, hm_ref, og_ref, gmh_ref, wa_ref, wm_ref, o_ref, *, a_transposed):
    if a_transposed:
        a = a_ref[0].astype(f32).T.astype(bf16)
    else:
        a = a_ref[0]
    hm = hm_ref[0]
    parts = []
    for hd in range(ML_HEADS):
        t = hm[:, hd * ML_DH:(hd + 1) * ML_DH]
        parts.append(t * lax.rsqrt(jnp.mean(t * t, axis=-1, keepdims=True) + EPS))
    hn = jnp.concatenate(parts, axis=1) * gmh_ref[...] * _sigmoid(og_ref[0])
    o_ref[0] = x_ref[0] + _dot(a, wa_ref[...]) + _dot(hn.astype(bf16), wm_ref[...])


def _outproj(x, a, hm, og, W, a_transposed):
    bv, sv, _ = x.shape
    tm = min(ROW_TILE, sv)
    hv = MLA_HEADS * V_DIM
    row = lambda w: pl.BlockSpec((1, tm, w), lambda b, j: (b, j, 0))
    a_spec = pl.BlockSpec((1, hv, tm), lambda b, j: (b, 0, j)) if a_transposed else row(hv)
    return pl.pallas_call(
        functools.partial(_outproj_kernel, a_transposed=a_transposed),
        grid=(bv, sv // tm),
        in_specs=[row(D_MODEL), a_spec, row(ML_WIDTH), row(ML_WIDTH), _const_spec((1, ML_WIDTH)),
                  _const_spec((hv, D_MODEL)), _const_spec((ML_WIDTH, D_MODEL))],
        out_specs=row(D_MODEL),
        out_shape=jax.ShapeDtypeStruct(x.shape, f32),
        compiler_params=_params(("parallel", "parallel"), 32),
        name="outproj_t" if a_transposed else "outproj",
    )(x, a, hm, og, W["g_mh"], W["w_out_a"], W["w_out_m"])


def _xattn_kernel(x_ref, mk_ref, mv_ref, gx_ref, wq_ref, gxq_ref, wo_ref, o_ref):
    x = x_ref[0]
    qx = _dot(_rms(x, gx_ref[...]).astype(bf16), wq_ref[...])
    mk = mk_ref[0].astype(bf16)
    mv = mv_ref[0].astype(bf16)
    gxq = gxq_ref[...]
    outs = []
    for hd in range(XA_HEADS):
        hs = slice(hd * XA_DH, (hd + 1) * XA_DH)
        t = _rms(qx[:, hs], gxq) * XA_SCALE
        s = _dot_nt(t.astype(bf16), mk[:, hs])
        pm = jnp.exp(s - jnp.max(s, axis=-1, keepdims=True))
        outs.append(_dot(pm.astype(bf16), mv[:, hs]) / jnp.sum(pm, axis=-1, keepdims=True))
    ox = jnp.concatenate(outs, axis=1)
    o_ref[0] = x + _dot(ox.astype(bf16), wo_ref[...])


def _xattn(x, mem_k, mem_v, W):
    bv, sv, _ = x.shape
    tm = min(ROW_TILE, sv)
    n_mem = mem_k.shape[1]
    row = pl.BlockSpec((1, tm, D_MODEL), lambda b, j: (b, j, 0))
    mem = pl.BlockSpec((1, n_mem, D_MODEL), lambda b, j: (b, 0, 0))
    return pl.pallas_call(
        _xattn_kernel,
        grid=(bv, sv // tm),
        in_specs=[row, mem, mem, _const_spec((1, D_MODEL)), _const_spec((D_MODEL, D_MODEL)),
                  _const_spec((1, XA_DH)), _const_spec((D_MODEL, D_MODEL))],
        out_specs=row,
        out_shape=jax.ShapeDtypeStruct(x.shape, f32),
        compiler_params=_params(("parallel", "parallel"), 40),
        name="xattn",
    )(x, mem_k, mem_v, W["g_xattn"], W["w_xq"], W["g_xq"], W["w_xo"])


def _mlp_kernel(x_ref, g_ref, w1_ref, w2_ref, o_ref):
    x = x_ref[0]
    hf = _rms(x, g_ref[...]).astype(bf16)
    acc = x
    for c in range(D_FF // D_MODEL):
        cs = slice(c * D_MODEL, (c + 1) * D_MODEL)
        u = jnp.square(jnp.maximum(_dot(hf, w1_ref[:, cs]), 0.0))
        acc = acc + _dot(u.astype(bf16), w2_ref[cs, :])
    o_ref[0] = acc


def _mlp(x, W):
    bv, sv, _ = x.shape
    tm = min(ROW_TILE, sv)
    row = pl.BlockSpec((1, tm, D_MODEL), lambda b, j: (b, j, 0))
    return pl.pallas_call(
        _mlp_kernel,
        grid=(bv, sv // tm),
        in_specs=[row, _const_spec((1, D_MODEL)), _const_spec((D_MODEL, D_FF)), _const_spec((D_FF, D_MODEL))],
        out_specs=row,
        out_shape=jax.ShapeDtypeStruct(x.shape, f32),
        compiler_params=_params(("parallel", "parallel"), 48),
        name="mlp",
    )(x, W["g_mlp"], W["w_ff1"], W["w_ff2"])


def _memkv_kernel(mem_ref, gm_ref, wk_ref, wv_ref, gk_ref, k_ref, v_ref):
    hm = _rms(mem_ref[0], gm_ref[...]).astype(bf16)
    k = _dot(hm, wk_ref[...])
    gk = gk_ref[...]
    k_ref[0] = jnp.concatenate([_rms(k[:, hd * XA_DH:(hd + 1) * XA_DH], gk) for hd in range(XA_HEADS)], axis=1)
    v_ref[0] = _dot(hm, wv_ref[...])


def _memkv(mem, W):
    bv, n_mem, _ = mem.shape
    blk = pl.BlockSpec((1, n_mem, D_MODEL), lambda b: (b, 0, 0))
    return pl.pallas_call(
        _memkv_kernel,
        grid=(bv,),
        in_specs=[blk, _const_spec((1, D_MODEL)), _const_spec((D_MODEL, D_MODEL)), _const_spec((D_MODEL, D_MODEL)),
                  _const_spec((1, XA_DH))],
        out_specs=[blk, blk],
        out_shape=[jax.ShapeDtypeStruct(mem.shape, f32)] * 2,
        compiler_params=_params(("parallel",), 32),
        name="memkv",
    )(mem, W["g_mem"], W["w_xk"], W["w_xv"], W["g_xk"])


def _pad_lanes(a, lo, width=LANE):
    pad = [(0, 0)] * (a.ndim - 1) + [(lo, width - lo - a.shape[-1])]
    return jnp.pad(a, pad)


def _rope_tables(pos):
    half = ROPE_DIM // 2
    inv_freq = ROPE_THETA ** (-jnp.arange(half, dtype=f32) / half)
    ang = pos.astype(f32)[:, None] * inv_freq[None, :]
    cos, sin = jnp.cos(ang), jnp.sin(ang)
    n = pos.shape[0]
    ones, zeros = jnp.ones((n, NOPE_DIM), f32), jnp.zeros((n, half), f32)
    tail = jnp.zeros((n, LANE - QK_DIM), f32)
    cos_t = jnp.concatenate([ones, cos, cos, tail], axis=1)
    sin_a = jnp.concatenate([0 * ones, -sin, zeros, tail], axis=1)
    sin_b = jnp.concatenate([0 * ones, zeros, sin, tail], axis=1)
    return cos_t, sin_a, sin_b


def _layer_weights(l, g_mix, w_in, g_qa, w_q_up, g_qnorm, g_kva, w_kv_up, g_knorm, w_conv, b_conv, b_igate, b_fgate,
                   g_mhead, w_out, g_xattn, g_mem, w_xq, w_xk, w_xv, g_xq, g_xk, w_xo, g_mlp, w_ff1, w_ff2):
    wi = w_in[l]
    off_kr = Q_RANK + KV_RANK
    off_mqk = off_kr + ROPE_DIM
    off_mv = off_mqk + 2 * ML_WIDTH
    off_mi = off_mv + ML_WIDTH
    off_mo = off_mi + 2 * ML_HEADS
    packed = jnp.concatenate([
        wi[:, :off_kr],
        _pad_lanes(wi[:, off_kr:off_mqk], NOPE_DIM),
        wi[:, off_mqk:off_mi],
        wi[:, off_mo:],
        _pad_lanes(wi[:, off_mi:off_mo], 0),
    ], axis=1)
    wq = w_q_up[l].reshape(Q_RANK, MLA_HEADS, QK_DIM)
    wkv = w_kv_up[l].reshape(KV_RANK, MLA_HEADS, NOPE_DIM + V_DIM)
    wv = wkv[:, :, NOPE_DIM:].reshape(KV_RANK, MLA_HEADS * V_DIM)
    row = lambda a: a.reshape(1, -1)
    return {
        "g_mix": row(g_mix[l]), "w_in": packed.astype(bf16), "g_qa": row(g_qa[l]),
        "w_q": _pad_lanes(wq, 0).reshape(Q_RANK, MLA_HEADS * LANE).astype(bf16),
        "g_qn": _pad_lanes(row(g_qnorm[l]), 0), "g_kva": row(g_kva[l]),
        "w_k": _pad_lanes(wkv[:, :, :NOPE_DIM], 0).reshape(KV_RANK, MLA_HEADS * LANE).astype(bf16),
        "w_v": wv.astype(bf16), "w_vT": wv.T.astype(bf16),
        "g_kn": _pad_lanes(row(g_knorm[l]), 0),
        "w_conv": w_conv[l], "b_conv": row(b_conv[l]),
        "b_gate": _pad_lanes(row(jnp.concatenate([b_igate[l], b_fgate[l]])), 0),
        "g_mh": row(g_mhead[l]),
        "w_out_a": w_out[l][:MLA_HEADS * V_DIM].astype(bf16), "w_out_m": w_out[l][MLA_HEADS * V_DIM:].astype(bf16),
        "g_xattn": row(g_xattn[l]), "g_mem": row(g_mem[l]),
        "w_xq": w_xq[l].astype(bf16), "w_xk": w_xk[l].astype(bf16), "w_xv": w_xv[l].astype(bf16),
        "g_xq": row(g_xq[l]), "g_xk": row(g_xk[l]), "w_xo": w_xo[l].astype(bf16),
        "g_mlp": row(g_mlp[l]), "w_ff1": w_ff1[l].astype(bf16), "w_ff2": w_ff2[l].astype(bf16),
    }


def _unpack_state(c_new, n_new, m_new, conv_new):
    return (c_new, n_new, m_new[:, 0, GATE_F:GATE_F + ML_HEADS], conv_new[:, SUBLANE - (CONV_W - 1):, :])


def kernel(x_prompt, x_sample, cache_mla_ckv, cache_mla_krope, state_mlstm_C, state_mlstm_n, state_mlstm_m,
           state_mlstm_conv, cache_mem_k, cache_mem_v, mem_prompt, g_mix, w_in, g_qa, w_q_up, g_qnorm, g_kva,
           w_kv_up, g_knorm, w_conv, b_conv, b_igate, b_fgate, g_mhead, w_out, g_xattn, g_mem, w_xq, w_xk, w_xv,
           g_xq, g_xk, w_xo, g_mlp, w_ff1, w_ff2):
    depth = w_in.shape[0]
    bp, sp, _ = x_prompt.shape
    bs, ss, _ = x_sample.shape
    past = cache_mla_ckv.shape[2]
    n_mem = mem_prompt.shape[1]
    weights = (g_mix, w_in, g_qa, w_q_up, g_qnorm, g_kva, w_kv_up, g_knorm, w_conv, b_conv, b_igate, b_fgate,
               g_mhead, w_out, g_xattn, g_mem, w_xq, w_xk, w_xv, g_xq, g_xk, w_xo, g_mlp, w_ff1, w_ff2)

    tabs_p = _rope_tables(jnp.arange(sp))
    tabs_s = tuple(jnp.tile(t, (bs, 1)) for t in _rope_tables(past + jnp.arange(ss)))
    ml_chunk = min(ML_CHUNK, sp)

    xp, xs = x_prompt, x_sample.reshape(1, bs * ss, D_MODEL)
    p_out = [[] for _ in range(8)]
    s_out = [[] for _ in range(6)]
    for l in range(depth):
        W = _layer_weights(l, *weights)

        mem_k, mem_v = _memkv(mem_prompt, W)
        q, ckv, kr, mqk, mv, og, gate, qn2 = _inproj(xp, tabs_p, W)
        k, vt, kn2 = _kvup(ckv, kr, W, True)
        safe = jnp.max(qn2) * jnp.max(kn2) * 1.05 <= SAFE_LOG2_SCORE ** 2
        a = _mla_prompt(q, k, vt, jnp.logical_not(safe).astype(jnp.int32).reshape(1))
        zeros = lambda *s: jnp.zeros((bp,) + s, f32)
        hm, *state = _mlstm(mqk, mv, gate, zeros(SUBLANE, 2 * ML_WIDTH), zeros(ML_HEADS, ML_DH, ML_DH),
                            zeros(ML_HEADS, ML_DH), zeros(1, LANE), W, ml_chunk)
        xp = _outproj(xp, a, hm, og, W, True)
        xp = _xattn(xp, mem_k, mem_v, W)
        xp = _mlp(xp, W)
        new = (ckv, kr[..., NOPE_DIM:QK_DIM]) + _unpack_state(*state) + (
            mem_k.reshape(bp, n_mem, XA_HEADS, XA_DH), mem_v.reshape(bp, n_mem, XA_HEADS, XA_DH))
        for lst, t in zip(p_out, new):
            lst.append(t)

        q, ckv, kr, mqk, mv, og, gate, _ = _inproj(xs, tabs_s, W)
        per_stream = lambda t: t.reshape(bs, ss, t.shape[-1])
        ckv, kr = per_stream(ckv), per_stream(kr)
        kn, vn = _kvup(ckv, kr, W, False)
        kp, vp = _kvup(cache_mla_ckv[l], _pad_lanes(cache_mla_krope[l], NOPE_DIM), W, False)
        a = _mla_sample(q, kp, vp, kn, vn)
        conv0 = jnp.pad(state_mlstm_conv[l], ((0, 0), (SUBLANE - (CONV_W - 1), 0), (0, 0)))
        m0 = _pad_lanes(state_mlstm_m[l], GATE_F).reshape(bs, 1, LANE)
        hm, *state = _mlstm(per_stream(mqk), per_stream(mv), per_stream(gate), conv0, state_mlstm_C[l],
                            state_mlstm_n[l], m0, W, ss)
        flat = lambda t: t.reshape(1, bs * ss, t.shape[-1])
        xs = _outproj(xs, flat(a), flat(hm), og, W, False)
        xs = flat(_xattn(per_stream(xs), cache_mem_k[l].reshape(bs, n_mem, D_MODEL),
                         cache_mem_v[l].reshape(bs, n_mem, D_MODEL), W))
        xs = _mlp(xs, W)
        new = (ckv, kr[..., NOPE_DIM:QK_DIM]) + _unpack_state(*state)
        for lst, t in zip(s_out, new):
            lst.append(t)

    outs_p = tuple(jnp.stack(t) for t in p_out)
    outs_s = tuple(jnp.stack(t) for t in s_out)
    return (xp, xs.reshape(bs, ss, D_MODEL)) + outs_p + outs_s
```

```python
import functools

import jax
import jax.numpy as jnp
import numpy as np
from jax import lax
from jax.experimental import pallas as pl
from jax.experimental.pallas import tpu as pltpu

f32 = jnp.float32
bf16 = jnp.bfloat16

D_MODEL = 1024
CHUNK = 64
EPS = 1e-6
MLA_HEADS = 8
Q_RANK = 256
KV_RANK = 128
NOPE_DIM = 64
ROPE_DIM = 32
QK_DIM = NOPE_DIM + ROPE_DIM
V_DIM = 64
ROPE_THETA = 10000.0
MLA_SCALE = QK_DIM ** -0.5
ML_HEADS = 4
ML_DH = 128
ML_WIDTH = ML_HEADS * ML_DH
CONV_W = 4
XA_HEADS = 4
XA_DH = D_MODEL // XA_HEADS
XA_SCALE = XA_DH ** -0.5
D_FF = 4 * D_MODEL

LANE = 128
SUBLANE = 8
MIB = 1024 * 1024

C_QA = 0
C_KVA = C_QA + Q_RANK
C_KR = C_KVA + KV_RANK
C_MQK = C_KR + LANE
C_MV = C_MQK + 2 * ML_WIDTH
C_MO = C_MV + ML_WIDTH
C_GATE = C_MO + ML_WIDTH
IN_COLS_PACKED = C_GATE + LANE
GATE_F = ML_HEADS

ROW_TILE = 512
ATT_TILE = 2048
ATT_BLOCK = 1024
ATT_SUB = 512
ML_CHUNK = 256

BF16_ROWS = 16
V_EXT = V_DIM + BF16_ROWS
LOG2E = 1.4426950408889634
SAFE_LOG2_SCORE = 64.0

NT_DIMS = (((1,), (1,)), ((), ()))


def _rms(x, g):
    return x * lax.rsqrt(jnp.mean(x * x, axis=-1, keepdims=True) + EPS) * g


def _dot(a, b):
    return jnp.dot(a, b, preferred_element_type=f32)


def _dot_nt(a, b):
    return lax.dot_general(a, b, NT_DIMS, preferred_element_type=f32)


def _sigmoid(x):
    return 1.0 / (1.0 + jnp.exp(-x))


def _const_spec(shape):
    nd = len(shape)
    return pl.BlockSpec(shape, lambda *_: (0,) * nd, pipeline_mode=pl.Buffered(1))


def _params(semantics, vmem_mib):
    return pltpu.CompilerParams(dimension_semantics=semantics, vmem_limit_bytes=vmem_mib * MIB)


def _inproj_kernel(x_ref, cos_ref, sa_ref, sb_ref, gmix_ref, win_ref, gqa_ref, wq_ref, gqn_ref, gkva_ref,
                   q_ref, ckv_ref, kr_ref, mqk_ref, mv_ref, og_ref, gate_ref, qn_ref):
    x = x_ref[0]
    h = _rms(x, gmix_ref[...]).astype(bf16)
    proj = lambda lo, hi: _dot(h, win_ref[:, lo:hi])
    cos, sa, sb = cos_ref[...], sa_ref[...], sb_ref[...]

    def rope(t):
        return t * cos + pltpu.roll(t, LANE - ROPE_DIM // 2, 1) * sa + pltpu.roll(t, ROPE_DIM // 2, 1) * sb

    q_lat = _rms(proj(C_QA, C_KVA), gqa_ref[...])
    qf = _dot(q_lat.astype(bf16), wq_ref[...])
    gq = gqn_ref[...]
    norm2 = None
    for hd in range(MLA_HEADS):
        t = rope(qf[:, hd * LANE:(hd + 1) * LANE])
        ss = jnp.sum(t * t, axis=-1, keepdims=True) * (1.0 / QK_DIM)
        qh = t * lax.rsqrt(ss + EPS) * gq * (MLA_SCALE * LOG2E)
        q_ref[0, hd] = qh.astype(bf16)
        n2 = jnp.sum(qh * qh, axis=-1, keepdims=True)
        norm2 = n2 if norm2 is None else jnp.maximum(norm2, n2)
    qn_ref[0, 0] = jnp.broadcast_to(jnp.max(norm2, axis=0, keepdims=True), qn_ref.shape[2:])
    latent = proj(C_KVA, C_MQK)
    ckv_ref[0] = _rms(latent[:, :KV_RANK], gkva_ref[...])
    kr_ref[0] = rope(latent[:, KV_RANK:])
    mqk_ref[0] = proj(C_MQK, C_MV)
    mv_ref[0] = proj(C_MV, C_MO)
    og_ref[0] = proj(C_MO, C_GATE)
    gate_ref[0] = proj(C_GATE, IN_COLS_PACKED)


def _inproj(x, tabs, W):
    bv, sv, _ = x.shape
    tm = min(ROW_TILE, sv)
    row = lambda w: pl.BlockSpec((1, tm, w), lambda b, j: (b, j, 0))
    tab = pl.BlockSpec((tm, LANE), lambda b, j: (j, 0))
    sds = jax.ShapeDtypeStruct
    return pl.pallas_call(
        _inproj_kernel,
        grid=(bv, sv // tm),
        in_specs=[row(D_MODEL), tab, tab, tab,
                  _const_spec((1, D_MODEL)), _const_spec((D_MODEL, IN_COLS_PACKED)),
                  _const_spec((1, Q_RANK)), _const_spec((Q_RANK, MLA_HEADS * LANE)),
                  _const_spec((1, LANE)), _const_spec((1, KV_RANK))],
        out_specs=[pl.BlockSpec((1, MLA_HEADS, tm, LANE), lambda b, j: (b, 0, j, 0)),
                   row(KV_RANK), row(LANE), row(2 * ML_WIDTH), row(ML_WIDTH), row(ML_WIDTH), row(LANE),
                   pl.BlockSpec((1, 1, SUBLANE, LANE), lambda b, j: (b, j, 0, 0))],
        out_shape=[sds((bv, MLA_HEADS, sv, LANE), bf16), sds((bv, sv, KV_RANK), f32), sds((bv, sv, LANE), f32),
                   sds((bv, sv, 2 * ML_WIDTH), f32), sds((bv, sv, ML_WIDTH), f32), sds((bv, sv, ML_WIDTH), f32),
                   sds((bv, sv, LANE), f32), sds((bv, sv // tm, SUBLANE, LANE), f32)],
        compiler_params=_params(("parallel", "parallel"), 48),
        name="inproj",
    )(x, *tabs, W["g_mix"], W["w_in"], W["g_qa"], W["w_q"], W["g_qn"], W["g_kva"])


def _kvup_kernel(ckv_ref, kr_ref, wk_ref, wv_ref, gk_ref, k_ref, v_ref, *maybe_kn_ref, transposed_v):
    c = ckv_ref[0].astype(bf16)
    kn = _dot(c, wk_ref[...])
    kr = kr_ref[0]
    g = gk_ref[...]
    norm2 = None
    for hd in range(MLA_HEADS):
        t = kn[:, hd * LANE:(hd + 1) * LANE] + kr
        ss = jnp.sum(t * t, axis=-1, keepdims=True) * (1.0 / QK_DIM)
        kh = t * lax.rsqrt(ss + EPS) * g
        k_ref[0, hd] = kh.astype(bf16)
        n2 = jnp.sum(kh * kh, axis=-1, keepdims=True)
        norm2 = n2 if norm2 is None else jnp.maximum(norm2, n2)
    if transposed_v:
        vt = _dot_nt(wv_ref[...], c).astype(bf16)
        tm = vt.shape[1]
        ones_tile = (lax.broadcasted_iota(jnp.int32, (BF16_ROWS, tm), 0) == 0).astype(bf16)
        for hd in range(MLA_HEADS):
            v_ref[0, hd, 0:V_DIM, :] = vt[hd * V_DIM:(hd + 1) * V_DIM, :]
            v_ref[0, hd, V_DIM:V_EXT, :] = ones_tile
        kn_ref, = maybe_kn_ref
        kn_ref[0, 0] = jnp.broadcast_to(jnp.max(norm2, axis=0, keepdims=True), kn_ref.shape[2:])
    else:
        v_ref[0] = _dot(c, wv_ref[...]).astype(bf16)


def _kvup(ckv, kr, W, transposed_v, first=0, count=None):
    _, sv, _ = ckv.shape
    bv = ckv.shape[0] if count is None else count
    tm = min(ROW_TILE, sv)
    hv = MLA_HEADS * V_DIM
    row_in = lambda w: pl.BlockSpec((1, tm, w), lambda b, j: (first + b, j, 0))
    row = lambda w: pl.BlockSpec((1, tm, w), lambda b, j: (b, j, 0))
    out_specs = [pl.BlockSpec((1, MLA_HEADS, tm, LANE), lambda b, j: (b, 0, j, 0))]
    out_shape = [jax.ShapeDtypeStruct((bv, MLA_HEADS, sv, LANE), bf16)]
    if transposed_v:
        out_specs += [pl.BlockSpec((1, MLA_HEADS, V_EXT, tm), lambda b, j: (b, 0, 0, j)),
                      pl.BlockSpec((1, 1, SUBLANE, LANE), lambda b, j: (b, j, 0, 0))]
        out_shape += [jax.ShapeDtypeStruct((bv, MLA_HEADS, V_EXT, sv), bf16),
                      jax.ShapeDtypeStruct((bv, sv // tm, SUBLANE, LANE), f32)]
        wv = W["w_vT"]
    else:
        out_specs += [row(hv)]
        out_shape += [jax.ShapeDtypeStruct((bv, sv, hv), bf16)]
        wv = W["w_v"]
    return pl.pallas_call(
        functools.partial(_kvup_kernel, transposed_v=transposed_v),
        grid=(bv, sv // tm),
        in_specs=[row_in(KV_RANK), row_in(LANE), _const_spec((KV_RANK, MLA_HEADS * LANE)), _const_spec(wv.shape),
                  _const_spec((1, LANE))],
        out_specs=out_specs,
        out_shape=out_shape,
        compiler_params=_params(("parallel", "parallel"), 32),
        name="kvup_t" if transposed_v else "kvup",
    )(ckv, kr, W["w_k"], wv, W["g_kn"])


def _mla_prompt_kernel(qi_ref, ki_ref, slow_ref, q_ref, k_ref, vt_ref, o_ref, m_sc, acc_sc, *, tile, sub):
    p = pl.program_id(1)
    qi, ki = qi_ref[p], ki_ref[p]
    slow = slow_ref[0] != 0
    nsub = tile // sub
    blk = min(ATT_BLOCK, tile)
    blk_key_chunk = lax.broadcasted_iota(jnp.int32, (blk, 1), 0) // CHUNK
    blk_query_chunk = lax.broadcasted_iota(jnp.int32, (1, blk), 1) // CHUNK
    query_chunk = lax.broadcasted_iota(jnp.int32, (1, sub), 1) // CHUNK

    @pl.when(ki == 0)
    def _():
        m_sc[...] = jnp.full(m_sc.shape, -jnp.inf, f32)
        acc_sc[...] = jnp.zeros(acc_sc.shape, f32)

    def scores(hd, lo, hi, cols):
        return _dot_nt(k_ref[0, hd, lo:hi, :], q_ref[0, hd, cols, :])

    def pv(hd, lo, hi, pm):
        return _dot(vt_ref[0, hd, :, lo:hi], pm.astype(bf16))

    def finish(hd, cols, acc):
        row0 = hd * V_DIM if isinstance(hd, int) else pl.multiple_of(hd * V_DIM, V_DIM)
        o_ref[0, pl.ds(row0, V_DIM), cols] = (acc[:V_DIM] / acc[V_DIM:V_DIM + 1]).astype(o_ref.dtype)

    def fast_step(hd, diag):
        for qs in range(tile // blk):
            cols = slice(qs * blk, (qs + 1) * blk)
            acc = acc_sc[hd, :, cols]
            for kb in range(qs if diag else tile // blk):
                lo = kb * blk
                acc = acc + pv(hd, lo, lo + blk, jnp.exp2(scores(hd, lo, lo + blk, cols)))
            if diag:
                lo = qs * blk
                pm = jnp.where(blk_key_chunk <= blk_query_chunk, jnp.exp2(scores(hd, lo, lo + blk, cols)), 0.0)
                finish(hd, cols, acc + pv(hd, lo, lo + blk, pm))
            else:
                acc_sc[hd, :, cols] = acc

    def slow_step(hd, diag):
        for qs in range(nsub):
            cols = slice(qs * sub, (qs + 1) * sub)
            kv_len = (qs + 1) * sub if diag else tile
            s = scores(hd, 0, kv_len, cols)
            if diag:
                kc = lax.broadcasted_iota(jnp.int32, (kv_len, 1), 0) // CHUNK
                s = jnp.where(kc <= query_chunk + (qs * sub) // CHUNK, s, -jnp.inf)
            m_old = m_sc[hd, :, cols]
            m_new = jnp.maximum(m_old, jnp.max(s, axis=0, keepdims=True))
            acc = jnp.exp2(m_old - m_new) * acc_sc[hd, :, cols] + pv(hd, 0, kv_len, jnp.exp2(s - m_new))
            if diag:
                finish(hd, cols, acc)
            else:
                m_sc[hd, :, cols] = m_new
                acc_sc[hd, :, cols] = acc

    for diag, where in ((False, ki != qi), (True, ki == qi)):
        @pl.when(jnp.logical_and(jnp.logical_not(slow), where))
        def _(diag=diag):
            lax.fori_loop(0, MLA_HEADS, lambda hd, c: (fast_step(hd, diag), c)[1], 0, unroll=2)

        @pl.when(jnp.logical_and(slow, where))
        def _(diag=diag):
            lax.fori_loop(0, MLA_HEADS, lambda hd, c: (slow_step(hd, diag), c)[1], 0)


def _mla_prompt(q, k, vt, slow):
    bv, _, sv, _ = q.shape
    tile = min(ATT_TILE, sv)
    sub = min(ATT_SUB, tile)
    nq = sv // tile
    pairs = [(i, j) for i in range(nq) for j in range(i + 1)]
    qi = jnp.asarray(np.array([a for a, _ in pairs], np.int32))
    ki = jnp.asarray(np.array([b for _, b in pairs], np.int32))
    hv = MLA_HEADS * V_DIM
    grid_spec = pltpu.PrefetchScalarGridSpec(
        num_scalar_prefetch=3,
        grid=(bv, len(pairs)),
        in_specs=[pl.BlockSpec((1, MLA_HEADS, tile, LANE), lambda b, p, qi, ki, sl: (b, 0, qi[p], 0)),
                  pl.BlockSpec((1, MLA_HEADS, tile, LANE), lambda b, p, qi, ki, sl: (b, 0, ki[p], 0)),
                  pl.BlockSpec((1, MLA_HEADS, V_EXT, tile), lambda b, p, qi, ki, sl: (b, 0, 0, ki[p]))],
        out_specs=pl.BlockSpec((1, hv, tile), lambda b, p, qi, ki, sl: (b, 0, qi[p])),
        scratch_shapes=[pltpu.VMEM((MLA_HEADS, 1, tile), f32), pltpu.VMEM((MLA_HEADS, V_EXT, tile), f32)],
    )
    return pl.pallas_call(
        functools.partial(_mla_prompt_kernel, tile=tile, sub=sub),
        grid_spec=grid_spec,
        out_shape=jax.ShapeDtypeStruct((bv, hv, sv), bf16),
        compiler_params=_params(("parallel", "arbitrary"), 56),
        name="mla_prompt",
    )(qi, ki, slow, q, k, vt)


def _mla_sample_kernel(q_ref, kp_ref, vp_ref, kn_ref, vn_ref, o_ref):
    vp = vp_ref[0]
    vn = vn_ref[0]
    lane_head = lax.broadcasted_iota(jnp.int32, (1, MLA_HEADS * V_DIM), 1) // V_DIM
    out = jnp.zeros(o_ref.shape[1:], f32)
    for hd in range(MLA_HEADS):
        qh = q_ref[0, hd]
        s1 = _dot_nt(qh, kp_ref[0, hd])
        s2 = _dot_nt(qh, kn_ref[0, hd])
        m = jnp.maximum(jnp.max(s1, axis=-1, keepdims=True), jnp.max(s2, axis=-1, keepdims=True))
        p1 = jnp.exp2(s1 - m)
        p2 = jnp.exp2(s2 - m)
        l = jnp.sum(p1, axis=-1, keepdims=True) + jnp.sum(p2, axis=-1, keepdims=True)
        sel = lane_head == hd
        o = _dot(p1.astype(bf16), jnp.where(sel, vp, 0)) + _dot(p2.astype(bf16), jnp.where(sel, vn, 0))
        out = out + o / l
    o_ref[0] = out.astype(o_ref.dtype)


def _mla_sample(q, kp, vp, kn, vn):
    bv, _, sv, _ = kn.shape
    past = kp.shape[2]
    hv = MLA_HEADS * V_DIM
    return pl.pallas_call(
        _mla_sample_kernel,
        grid=(bv,),
        in_specs=[pl.BlockSpec((1, MLA_HEADS, sv, LANE), lambda b: (0, 0, b, 0)),
                  pl.BlockSpec((1, MLA_HEADS, past, LANE), lambda b: (b, 0, 0, 0)),
                  pl.BlockSpec((1, past, hv), lambda b: (b, 0, 0)),
                  pl.BlockSpec((1, MLA_HEADS, sv, LANE), lambda b: (b, 0, 0, 0)),
                  pl.BlockSpec((1, sv, hv), lambda b: (b, 0, 0))],
        out_specs=pl.BlockSpec((1, sv, hv), lambda b: (b, 0, 0)),
        out_shape=jax.ShapeDtypeStruct((bv, sv, hv), bf16),
        compiler_params=_params(("parallel",), 32),
        name="mla_sample",
    )(q, kp, vp, kn, vn)


def _mlstm_kernel(mqk_ref, mv_ref, gate_ref, conv0_ref, c0_ref, n0_ref, m0_ref, wconv_ref, bconv_ref, bgate_ref,
                  h_ref, cout_ref, nout_ref, mout_ref, convout_ref, ext_sc, c_sc, n_sc, m_sc, *, chunk):
    step = pl.program_id(1)
    L = chunk

    @pl.when(step == 0)
    def _():
        ext_sc[0:SUBLANE, :] = conv0_ref[0]
        c_sc[...] = c0_ref[0]
        n_sc[...] = n0_ref[0]
        m_sc[...] = m0_ref[0]

    x = mqk_ref[0]
    ext_sc[SUBLANE:SUBLANE + L, :] = x
    y = bconv_ref[...] + x * wconv_ref[CONV_W - 1:CONV_W, :]
    for j in range(CONV_W - 1):
        off = SUBLANE - (CONV_W - 1) + j
        y = y + ext_sc[off:off + L, :] * wconv_ref[j:j + 1, :]
    ext_sc[0:SUBLANE, :] = x[L - SUBLANE:L, :]
    qk = y * _sigmoid(y)
    mq = qk[:, :ML_WIDTH]
    mk = qk[:, ML_WIDTH:] * (ML_DH ** -0.5)
    mv = mv_ref[0]

    g = gate_ref[0] + bgate_ref[...]
    ls = jnp.minimum(g, 0.0) - jnp.log1p(jnp.exp(-jnp.abs(g)))
    r_i = lax.broadcasted_iota(jnp.int32, (L, L), 0)
    c_i = lax.broadcasted_iota(jnp.int32, (L, L), 1)
    causal = c_i <= r_i
    tri = causal.astype(f32)
    bcum = jnp.dot(tri, ls, preferred_element_type=f32, precision=lax.Precision.HIGHEST)
    g_t = g.T[0:SUBLANE, :]
    bcum_t = lax.dot_general(ls.T[0:SUBLANE, :], tri, NT_DIMS, preferred_element_type=f32,
                             precision=lax.Precision.HIGHEST)

    lane = lax.broadcasted_iota(jnp.int32, (1, LANE), 1)
    sub8 = lax.broadcasted_iota(jnp.int32, (SUBLANE, 1), 0)
    last_row = lax.broadcasted_iota(jnp.int32, (L, 1), 0) == L - 1

    def col(t, idx):
        return jnp.sum(jnp.where(lane == idx, t, 0.0), axis=1, keepdims=True)

    def row(t, idx):
        return jnp.sum(jnp.where(sub8 == idx, t, 0.0), axis=0, keepdims=True)

    m_vec = m_sc[...]
    m_next = jnp.zeros_like(m_vec)
    for hd in range(ML_HEADS):
        hs = slice(hd * ML_DH, (hd + 1) * ML_DH)
        b_col, ig_col = col(bcum, GATE_F + hd), col(g, hd)
        b_row, ig_row = row(bcum_t, GATE_F + hd), row(g_t, hd)
        m_prev = col(m_vec, GATE_F + hd)
        inter = b_col + m_prev
        d = jnp.where(causal, b_col - b_row + ig_row, -jnp.inf)
        m_t = jnp.maximum(inter, jnp.max(d, axis=1, keepdims=True))
        w = jnp.exp(d - m_t)
        a_inter = jnp.exp(inter - m_t)
        qh, kh, vh = mq[:, hs], mk[:, hs], mv[:, hs]
        qb, vb = qh.astype(bf16), vh.astype(bf16)
        sqk = _dot_nt(qb, kh.astype(bf16)) * w
        c_h = c_sc[hd]
        n_h = n_sc[hd:hd + 1, :]
        num = a_inter * _dot(qb, c_h.astype(bf16)) + _dot(sqk.astype(bf16), vb)
        qn = a_inter * jnp.sum(qh * n_h, axis=1, keepdims=True) + jnp.sum(sqk, axis=1, keepdims=True)
        h_ref[0, :, hs] = num / jnp.maximum(jnp.abs(qn), jnp.exp(-m_t))
        b_last = jnp.sum(jnp.where(last_row, b_col, 0.0), axis=0, keepdims=True)
        m_end = jnp.sum(jnp.where(last_row, m_t, 0.0), axis=0, keepdims=True)
        decay = jnp.exp(b_last + m_prev - m_end)
        kw = kh * jnp.exp(b_last - b_col + ig_col - m_end)
        c_sc[hd] = decay * c_h + _dot(kw.T.astype(bf16), vb)
        n_sc[hd:hd + 1, :] = decay * n_h + jnp.sum(kw, axis=0, keepdims=True)
        m_next = m_next + jnp.where(lane == GATE_F + hd, m_end, 0.0)
    m_sc[...] = m_next

    @pl.when(step == pl.num_programs(1) - 1)
    def _():
        cout_ref[0] = c_sc[...]
        nout_ref[0] = n_sc[...]
        mout_ref[0] = m_sc[...]
        convout_ref[0] = ext_sc[0:SUBLANE, :]


def _mlstm(mqk, mv, gate, conv0, c0, n0, m0, W, chunk, c_first=0):
    bv, sv, _ = mqk.shape
    step = lambda w: pl.BlockSpec((1, chunk, w), lambda b, c: (b, c, 0))
    per_b = lambda *s: pl.BlockSpec((1,) + s, lambda b, c: (b,) + (0,) * len(s))
    c0_spec = pl.BlockSpec((1, ML_HEADS, ML_DH, ML_DH), lambda b, c: (c_first + b, 0, 0, 0))
    sds = jax.ShapeDtypeStruct
    return pl.pallas_call(
        functools.partial(_mlstm_kernel, chunk=chunk),
        grid=(bv, sv // chunk),
        in_specs=[step(2 * ML_WIDTH), step(ML_WIDTH), step(LANE),
                  per_b(SUBLANE, 2 * ML_WIDTH), c0_spec, per_b(ML_HEADS, ML_DH), per_b(1, LANE),
                  _const_spec((CONV_W, 2 * ML_WIDTH)), _const_spec((1, 2 * ML_WIDTH)), _const_spec((1, LANE))],
        out_specs=[step(ML_WIDTH), per_b(ML_HEADS, ML_DH, ML_DH), per_b(ML_HEADS, ML_DH), per_b(1, LANE),
                   per_b(SUBLANE, 2 * ML_WIDTH)],
        out_shape=[sds((bv, sv, ML_WIDTH), f32), sds((bv, ML_HEADS, ML_DH, ML_DH), f32), sds((bv, ML_HEADS, ML_DH), f32),
                   sds((bv, 1, LANE), f32), sds((bv, SUBLANE, 2 * ML_WIDTH), f32)],
        scratch_shapes=[pltpu.VMEM((SUBLANE + chunk, 2 * ML_WIDTH), f32), pltpu.VMEM((ML_HEADS, ML_DH, ML_DH), f32),
                        pltpu.VMEM((ML_HEADS, ML_DH), f32), pltpu.VMEM((1, LANE), f32)],
        compiler_params=_params(("parallel", "arbitrary"), 32),
        name="mlstm",
    )(mqk, mv, gate, conv0, c0, n0, m0, W["w_conv"], W["b_conv"], W["b_gate"])


def _outproj_kernel(x_ref, a_ref, hm_ref, og_ref, gmh_ref, wa_ref, wm_ref, o_ref, *, a_transposed):
    if a_transposed:
        a = a_ref[0].astype(f32).T.astype(bf16)
    else:
        a = a_ref[0]
    hm = hm_ref[0]
    parts = []
    for hd in range(ML_HEADS):
        t = hm[:, hd * ML_DH:(hd + 1) * ML_DH]
        parts.append(t * lax.rsqrt(jnp.mean(t * t, axis=-1, keepdims=True) + EPS))
    hn = jnp.concatenate(parts, axis=1) * gmh_ref[...] * _sigmoid(og_ref[0])
    o_ref[0] = x_ref[0] + _dot(a, wa_ref[...]) + _dot(hn.astype(bf16), wm_ref[...])


def _outproj(x, a, hm, og, W, a_transposed):
    bv, sv, _ = x.shape
    tm = min(ROW_TILE, sv)
    hv = MLA_HEADS * V_DIM
    row = lambda w: pl.BlockSpec((1, tm, w), lambda b, j: (b, j, 0))
    a_spec = pl.BlockSpec((1, hv, tm), lambda b, j: (b, 0, j)) if a_transposed else row(hv)
    return pl.pallas_call(
        functools.partial(_outproj_kernel, a_transposed=a_transposed),
        grid=(bv, sv // tm),
        in_specs=[row(D_MODEL), a_spec, row(ML_WIDTH), row(ML_WIDTH), _const_spec((1, ML_WIDTH)),
                  _const_spec((hv, D_MODEL)), _const_spec((ML_WIDTH, D_MODEL))],
        out_specs=row(D_MODEL),
        out_shape=jax.ShapeDtypeStruct(x.shape, f32),
        compiler_params=_params(("parallel", "parallel"), 32),
        name="outproj_t" if a_transposed else "outproj",
    )(x, a, hm, og, W["g_mh"], W["w_out_a"], W["w_out_m"])


def _xattn_kernel(x_ref, mk_ref, mv_ref, gx_ref, wq_ref, gxq_ref, wo_ref, o_ref):
    x = x_ref[0]
    qx = _dot(_rms(x, gx_ref[...]).astype(bf16), wq_ref[...])
    mk = mk_ref[0].astype(bf16)
    mv = mv_ref[0].astype(bf16)
    gxq = gxq_ref[...]
    outs = []
    for hd in range(XA_HEADS):
        hs = slice(hd * XA_DH, (hd + 1) * XA_DH)
        t = _rms(qx[:, hs], gxq) * XA_SCALE
        s = _dot_nt(t.astype(bf16), mk[:, hs])
        pm = jnp.exp(s - jnp.max(s, axis=-1, keepdims=True))
        outs.append(_dot(pm.astype(bf16), mv[:, hs]) / jnp.sum(pm, axis=-1, keepdims=True))
    ox = jnp.concatenate(outs, axis=1)
    o_ref[0] = x + _dot(ox.astype(bf16), wo_ref[...])


def _xattn(x, mem_k, mem_v, W, mem_first=0):
    bv, sv, _ = x.shape
    tm = min(ROW_TILE, sv)
    n_mem = mem_k.shape[1]
    row = pl.BlockSpec((1, tm, D_MODEL), lambda b, j: (b, j, 0))
    mem = pl.BlockSpec((1, n_mem, D_MODEL), lambda b, j: (mem_first + b, 0, 0))
    return pl.pallas_call(
        _xattn_kernel,
        grid=(bv, sv // tm),
        in_specs=[row, mem, mem, _const_spec((1, D_MODEL)), _const_spec((D_MODEL, D_MODEL)),
                  _const_spec((1, XA_DH)), _const_spec((D_MODEL, D_MODEL))],
        out_specs=row,
        out_shape=jax.ShapeDtypeStruct(x.shape, f32),
        compiler_params=_params(("parallel", "parallel"), 40),
        name="xattn",
    )(x, mem_k, mem_v, W["g_xattn"], W["w_xq"], W["g_xq"], W["w_xo"])


def _mlp_kernel(x_ref, g_ref, w1_ref, w2_ref, o_ref):
    x = x_ref[0]
    hf = _rms(x, g_ref[...]).astype(bf16)
    acc = x
    for c in range(D_FF // D_MODEL):
        cs = slice(c * D_MODEL, (c + 1) * D_MODEL)
        u = jnp.square(jnp.maximum(_dot(hf, w1_ref[:, cs]), 0.0))
        acc = acc + _dot(u.astype(bf16), w2_ref[cs, :])
    o_ref[0] = acc


def _mlp(x, W):
    bv, sv, _ = x.shape
    tm = min(ROW_TILE, sv)
    row = pl.BlockSpec((1, tm, D_MODEL), lambda b, j: (b, j, 0))
    return pl.pallas_call(
        _mlp_kernel,
        grid=(bv, sv // tm),
        in_specs=[row, _const_spec((1, D_MODEL)), _const_spec((D_MODEL, D_FF)), _const_spec((D_FF, D_MODEL))],
        out_specs=row,
        out_shape=jax.ShapeDtypeStruct(x.shape, f32),
        compiler_params=_params(("parallel", "parallel"), 48),
        name="mlp",
    )(x, W["g_mlp"], W["w_ff1"], W["w_ff2"])


def _memkv_kernel(mem_ref, gm_ref, wk_ref, wv_ref, gk_ref, k_ref, v_ref):
    hm = _rms(mem_ref[0], gm_ref[...]).astype(bf16)
    k = _dot(hm, wk_ref[...])
    gk = gk_ref[...]
    k_ref[0] = jnp.concatenate([_rms(k[:, hd * XA_DH:(hd + 1) * XA_DH], gk) for hd in range(XA_HEADS)], axis=1)
    v_ref[0] = _dot(hm, wv_ref[...])


def _memkv(mem, W):
    bv, n_mem, _ = mem.shape
    blk = pl.BlockSpec((1, n_mem, D_MODEL), lambda b: (b, 0, 0))
    return pl.pallas_call(
        _memkv_kernel,
        grid=(bv,),
        in_specs=[blk, _const_spec((1, D_MODEL)), _const_spec((D_MODEL, D_MODEL)), _const_spec((D_MODEL, D_MODEL)),
                  _const_spec((1, XA_DH))],
        out_specs=[blk, blk],
        out_shape=[jax.ShapeDtypeStruct(mem.shape, f32)] * 2,
        compiler_params=_params(("parallel",), 32),
        name="memkv",
    )(mem, W["g_mem"], W["w_xk"], W["w_xv"], W["g_xk"])


def _pad_lanes(a, lo, width=LANE):
    pad = [(0, 0)] * (a.ndim - 1) + [(lo, width - lo - a.shape[-1])]
    return jnp.pad(a, pad)


def _rope_tables(pos):
    half = ROPE_DIM // 2
    inv_freq = ROPE_THETA ** (-jnp.arange(half, dtype=f32) / half)
    ang = pos.astype(f32)[:, None] * inv_freq[None, :]
    cos, sin = jnp.cos(ang), jnp.sin(ang)
    n = pos.shape[0]
    ones, zeros = jnp.ones((n, NOPE_DIM), f32), jnp.zeros((n, half), f32)
    tail = jnp.zeros((n, LANE - QK_DIM), f32)
    cos_t = jnp.concatenate([ones, cos, cos, tail], axis=1)
    sin_a = jnp.concatenate([0 * ones, -sin, zeros, tail], axis=1)
    sin_b = jnp.concatenate([0 * ones, zeros, sin, tail], axis=1)
    return cos_t, sin_a, sin_b


def _layer_weights(l, g_mix, w_in, g_qa, w_q_up, g_qnorm, g_kva, w_kv_up, g_knorm, w_conv, b_conv, b_igate, b_fgate,
                   g_mhead, w_out, g_xattn, g_mem, w_xq, w_xk, w_xv, g_xq, g_xk, w_xo, g_mlp, w_ff1, w_ff2):
    wi = w_in[l]
    off_kr = Q_RANK + KV_RANK
    off_mqk = off_kr + ROPE_DIM
    off_mv = off_mqk + 2 * ML_WIDTH
    off_mi = off_mv + ML_WIDTH
    off_mo = off_mi + 2 * ML_HEADS
    packed = jnp.concatenate([
        wi[:, :off_kr],
        _pad_lanes(wi[:, off_kr:off_mqk], NOPE_DIM),
        wi[:, off_mqk:off_mi],
        wi[:, off_mo:],
        _pad_lanes(wi[:, off_mi:off_mo], 0),
    ], axis=1)
    wq = w_q_up[l].reshape(Q_RANK, MLA_HEADS, QK_DIM)
    wkv = w_kv_up[l].reshape(KV_RANK, MLA_HEADS, NOPE_DIM + V_DIM)
    wv = wkv[:, :, NOPE_DIM:].reshape(KV_RANK, MLA_HEADS * V_DIM)
    row = lambda a: a.reshape(1, -1)
    return {
        "g_mix": row(g_mix[l]), "w_in": packed.astype(bf16), "g_qa": row(g_qa[l]),
        "w_q": _pad_lanes(wq, 0).reshape(Q_RANK, MLA_HEADS * LANE).astype(bf16),
        "g_qn": _pad_lanes(row(g_qnorm[l]), 0), "g_kva": row(g_kva[l]),
        "w_k": _pad_lanes(wkv[:, :, :NOPE_DIM], 0).reshape(KV_RANK, MLA_HEADS * LANE).astype(bf16),
        "w_v": wv.astype(bf16), "w_vT": wv.T.astype(bf16),
        "g_kn": _pad_lanes(row(g_knorm[l]), 0),
        "w_conv": w_conv[l], "b_conv": row(b_conv[l]),
        "b_gate": _pad_lanes(row(jnp.concatenate([b_igate[l], b_fgate[l]])), 0),
        "g_mh": row(g_mhead[l]),
        "w_out_a": w_out[l][:MLA_HEADS * V_DIM].astype(bf16), "w_out_m": w_out[l][MLA_HEADS * V_DIM:].astype(bf16),
        "g_xattn": row(g_xattn[l]), "g_mem": row(g_mem[l]),
        "w_xq": w_xq[l].astype(bf16), "w_xk": w_xk[l].astype(bf16), "w_xv": w_xv[l].astype(bf16),
        "g_xq": row(g_xq[l]), "g_xk": row(g_xk[l]), "w_xo": w_xo[l].astype(bf16),
        "g_mlp": row(g_mlp[l]), "w_ff1": w_ff1[l].astype(bf16), "w_ff2": w_ff2[l].astype(bf16),
    }


def _unpack_state(c_new, n_new, m_new, conv_new):
    return (c_new, n_new, m_new[:, 0, GATE_F:GATE_F + ML_HEADS], conv_new[:, SUBLANE - (CONV_W - 1):, :])


def kernel(x_prompt, x_sample, cache_mla_ckv, cache_mla_krope, state_mlstm_C, state_mlstm_n, state_mlstm_m,
           state_mlstm_conv, cache_mem_k, cache_mem_v, mem_prompt, g_mix, w_in, g_qa, w_q_up, g_qnorm, g_kva,
           w_kv_up, g_knorm, w_conv, b_conv, b_igate, b_fgate, g_mhead, w_out, g_xattn, g_mem, w_xq, w_xk, w_xv,
           g_xq, g_xk, w_xo, g_mlp, w_ff1, w_ff2):
    depth = w_in.shape[0]
    bp, sp, _ = x_prompt.shape
    bs, ss, _ = x_sample.shape
    past = cache_mla_ckv.shape[2]
    n_mem = mem_prompt.shape[1]
    weights = (g_mix, w_in, g_qa, w_q_up, g_qnorm, g_kva, w_kv_up, g_knorm, w_conv, b_conv, b_igate, b_fgate,
               g_mhead, w_out, g_xattn, g_mem, w_xq, w_xk, w_xv, g_xq, g_xk, w_xo, g_mlp, w_ff1, w_ff2)

    tabs_p = _rope_tables(jnp.arange(sp))
    tabs_s = tuple(jnp.tile(t, (bs, 1)) for t in _rope_tables(past + jnp.arange(ss)))
    ml_chunk = min(ML_CHUNK, sp)
    cache_ckv = cache_mla_ckv.reshape(depth * bs, past, KV_RANK)
    cache_kr = _pad_lanes(cache_mla_krope, NOPE_DIM).reshape(depth * bs, past, LANE)
    cache_c = state_mlstm_C.reshape(depth * bs, ML_HEADS, ML_DH, ML_DH)
    cache_mk = cache_mem_k.reshape(depth * bs, n_mem, D_MODEL)
    cache_mv = cache_mem_v.reshape(depth * bs, n_mem, D_MODEL)

    xp, xs = x_prompt, x_sample.reshape(1, bs * ss, D_MODEL)
    p_out = [[] for _ in range(8)]
    s_out = [[] for _ in range(6)]
    for l in range(depth):
        W = _layer_weights(l, *weights)

        mem_k, mem_v = _memkv(mem_prompt, W)
        q, ckv, kr, mqk, mv, og, gate, qn2 = _inproj(xp, tabs_p, W)
        k, vt, kn2 = _kvup(ckv, kr, W, True)
        safe = jnp.max(qn2) * jnp.max(kn2) * 1.05 <= SAFE_LOG2_SCORE ** 2
        a = _mla_prompt(q, k, vt, jnp.logical_not(safe).astype(jnp.int32).reshape(1))
        zeros = lambda *s: jnp.zeros((bp,) + s, f32)
        hm, *state = _mlstm(mqk, mv, gate, zeros(SUBLANE, 2 * ML_WIDTH), zeros(ML_HEADS, ML_DH, ML_DH),
                            zeros(ML_HEADS, ML_DH), zeros(1, LANE), W, ml_chunk)
        xp = _outproj(xp, a, hm, og, W, True)
        xp = _xattn(xp, mem_k, mem_v, W)
        xp = _mlp(xp, W)
        new = (ckv, kr[..., NOPE_DIM:QK_DIM]) + _unpack_state(*state) + (
            mem_k.reshape(bp, n_mem, XA_HEADS, XA_DH), mem_v.reshape(bp, n_mem, XA_HEADS, XA_DH))
        for lst, t in zip(p_out, new):
            lst.append(t)

        q, ckv, kr, mqk, mv, og, gate, _ = _inproj(xs, tabs_s, W)
        per_stream = lambda t: t.reshape(bs, ss, t.shape[-1])
        ckv, kr = per_stream(ckv), per_stream(kr)
        kn, vn = _kvup(ckv, kr, W, False)
        kp, vp = _kvup(cache_ckv, cache_kr, W, False, first=l * bs, count=bs)
        a = _mla_sample(q, kp, vp, kn, vn)
        conv0 = jnp.pad(state_mlstm_conv[l], ((0, 0), (SUBLANE - (CONV_W - 1), 0), (0, 0)))
        m0 = _pad_lanes(state_mlstm_m[l], GATE_F).reshape(bs, 1, LANE)
        hm, *state = _mlstm(per_stream(mqk), per_stream(mv), per_stream(gate), conv0, cache_c,
                            state_mlstm_n[l], m0, W, ss, c_first=l * bs)
        flat = lambda t: t.reshape(1, bs * ss, t.shape[-1])
        xs = _outproj(xs, flat(a), flat(hm), og, W, False)
        xs = flat(_xattn(per_stream(xs), cache_mk, cache_mv, W, mem_first=l * bs))
        xs = _mlp(xs, W)
        new = (ckv, kr[..., NOPE_DIM:QK_DIM]) + _unpack_state(*state)
        for lst, t in zip(s_out, new):
            lst.append(t)

    outs_p = tuple(jnp.stack(t) for t in p_out)
    outs_s = tuple(jnp.stack(t) for t in s_out)
    return (xp, xs.reshape(bs, ss, D_MODEL)) + outs_p + outs_s
```

```python
import functools

import jax
import jax.numpy as jnp
import numpy as np
from jax import lax
from jax.experimental import pallas as pl
from jax.experimental.pallas import tpu as pltpu

f32 = jnp.float32
bf16 = jnp.bfloat16

D_MODEL = 1024
CHUNK = 64
EPS = 1e-6
MLA_HEADS = 8
Q_RANK = 256
KV_RANK = 128
NOPE_DIM = 64
ROPE_DIM = 32
QK_DIM = NOPE_DIM + ROPE_DIM
V_DIM = 64
ROPE_THETA = 10000.0
MLA_SCALE = QK_DIM ** -0.5
ML_HEADS = 4
ML_DH = 128
ML_WIDTH = ML_HEADS * ML_DH
CONV_W = 4
XA_HEADS = 4
XA_DH = D_MODEL // XA_HEADS
XA_SCALE = XA_DH ** -0.5
D_FF = 4 * D_MODEL

LANE = 128
SUBLANE = 8
MIB = 1024 * 1024

C_QA = 0
C_KVA = C_QA + Q_RANK
C_KR = C_KVA + KV_RANK
C_MQK = C_KR + LANE
C_MV = C_MQK + 2 * ML_WIDTH
C_MO = C_MV + ML_WIDTH
C_GATE = C_MO + ML_WIDTH
IN_COLS_PACKED = C_GATE + LANE
GATE_F = ML_HEADS

ROW_TILE = 512
ATT_TILE = 2048
ATT_BLOCK = 1024
ATT_SUB = 512
ML_CHUNK = 256

BF16_ROWS = 16
V_EXT = V_DIM + BF16_ROWS
LOG2E = 1.4426950408889634
SAFE_LOG2_SCORE = 64.0

NT_DIMS = (((1,), (1,)), ((), ()))


def _rms(x, g):
    return x * lax.rsqrt(jnp.mean(x * x, axis=-1, keepdims=True) + EPS) * g


def _dot(a, b):
    return jnp.dot(a, b, preferred_element_type=f32)


def _dot_nt(a, b):
    return lax.dot_general(a, b, NT_DIMS, preferred_element_type=f32)


def _sigmoid(x):
    return 1.0 / (1.0 + jnp.exp(-x))


def _const_spec(shape):
    nd = len(shape)
    return pl.BlockSpec(shape, lambda *_: (0,) * nd, pipeline_mode=pl.Buffered(1))


def _params(semantics, vmem_mib):
    return pltpu.CompilerParams(dimension_semantics=semantics, vmem_limit_bytes=vmem_mib * MIB)


def _inproj_kernel(x_ref, cos_ref, sa_ref, sb_ref, gmix_ref, win_ref, gqa_ref, wq_ref, gqn_ref, gkva_ref,
                   q_ref, ckv_ref, kr_ref, mqk_ref, mv_ref, og_ref, gate_ref, qn_ref):
    x = x_ref[0]
    h = _rms(x, gmix_ref[...]).astype(bf16)
    proj = lambda lo, hi: _dot(h, win_ref[:, lo:hi])
    cos, sa, sb = cos_ref[...], sa_ref[...], sb_ref[...]

    def rope(t):
        return t * cos + pltpu.roll(t, LANE - ROPE_DIM // 2, 1) * sa + pltpu.roll(t, ROPE_DIM // 2, 1) * sb

    q_lat = _rms(proj(C_QA, C_KVA), gqa_ref[...])
    qf = _dot(q_lat.astype(bf16), wq_ref[...])
    gq = gqn_ref[...]
    norm2 = None
    for hd in range(MLA_HEADS):
        t = rope(qf[:, hd * LANE:(hd + 1) * LANE])
        ss = jnp.sum(t * t, axis=-1, keepdims=True) * (1.0 / QK_DIM)
        qh = t * lax.rsqrt(ss + EPS) * gq * (MLA_SCALE * LOG2E)
        q_ref[0, hd] = qh.astype(bf16)
        n2 = jnp.sum(qh * qh, axis=-1, keepdims=True)
        norm2 = n2 if norm2 is None else jnp.maximum(norm2, n2)
    qn_ref[0, 0] = jnp.broadcast_to(jnp.max(norm2, axis=0, keepdims=True), qn_ref.shape[2:])
    latent = proj(C_KVA, C_MQK)
    ckv_ref[0] = _rms(latent[:, :KV_RANK], gkva_ref[...])
    kr_ref[0] = rope(latent[:, KV_RANK:])
    mqk_ref[0] = proj(C_MQK, C_MV)
    mv_ref[0] = proj(C_MV, C_MO)
    og_ref[0] = proj(C_MO, C_GATE)
    gate_ref[0] = proj(C_GATE, IN_COLS_PACKED)


def _inproj(x, tabs, W):
    bv, sv, _ = x.shape
    tm = min(ROW_TILE, sv)
    row = lambda w: pl.BlockSpec((1, tm, w), lambda b, j: (b, j, 0))
    tab = pl.BlockSpec((tm, LANE), lambda b, j: (j, 0))
    sds = jax.ShapeDtypeStruct
    return pl.pallas_call(
        _inproj_kernel,
        grid=(bv, sv // tm),
        in_specs=[row(D_MODEL), tab, tab, tab,
                  _const_spec((1, D_MODEL)), _const_spec((D_MODEL, IN_COLS_PACKED)),
                  _const_spec((1, Q_RANK)), _const_spec((Q_RANK, MLA_HEADS * LANE)),
                  _const_spec((1, LANE)), _const_spec((1, KV_RANK))],
        out_specs=[pl.BlockSpec((1, MLA_HEADS, tm, LANE), lambda b, j: (b, 0, j, 0)),
                   row(KV_RANK), row(LANE), row(2 * ML_WIDTH), row(ML_WIDTH), row(ML_WIDTH), row(LANE),
                   pl.BlockSpec((1, 1, SUBLANE, LANE), lambda b, j: (b, j, 0, 0))],
        out_shape=[sds((bv, MLA_HEADS, sv, LANE), bf16), sds((bv, sv, KV_RANK), f32), sds((bv, sv, LANE), f32),
                   sds((bv, sv, 2 * ML_WIDTH), f32), sds((bv, sv, ML_WIDTH), f32), sds((bv, sv, ML_WIDTH), f32),
                   sds((bv, sv, LANE), f32), sds((bv, sv // tm, SUBLANE, LANE), f32)],
        compiler_params=_params(("parallel", "parallel"), 48),
        name="inproj",
    )(x, *tabs, W["g_mix"], W["w_in"], W["g_qa"], W["w_q"], W["g_qn"], W["g_kva"])


def _kvup_kernel(ckv_ref, kr_ref, wk_ref, wv_ref, gk_ref, k_ref, v_ref, *maybe_kn_ref, transposed_v):
    c = ckv_ref[0].astype(bf16)
    kn = _dot(c, wk_ref[...])
    kr = kr_ref[0]
    g = gk_ref[...]
    norm2 = None
    for hd in range(MLA_HEADS):
        t = kn[:, hd * LANE:(hd + 1) * LANE] + kr
        ss = jnp.sum(t * t, axis=-1, keepdims=True) * (1.0 / QK_DIM)
        kh = t * lax.rsqrt(ss + EPS) * g
        k_ref[0, hd] = kh.astype(bf16)
        n2 = jnp.sum(kh * kh, axis=-1, keepdims=True)
        norm2 = n2 if norm2 is None else jnp.maximum(norm2, n2)
    if transposed_v:
        vt = _dot_nt(wv_ref[...], c).astype(bf16)
        tm = vt.shape[1]
        ones_tile = (lax.broadcasted_iota(jnp.int32, (BF16_ROWS, tm), 0) == 0).astype(bf16)
        for hd in range(MLA_HEADS):
            v_ref[0, hd, 0:V_DIM, :] = vt[hd * V_DIM:(hd + 1) * V_DIM, :]
            v_ref[0, hd, V_DIM:V_EXT, :] = ones_tile
        kn_ref, = maybe_kn_ref
        kn_ref[0, 0] = jnp.broadcast_to(jnp.max(norm2, axis=0, keepdims=True), kn_ref.shape[2:])
    else:
        v_ref[0] = _dot(c, wv_ref[...]).astype(bf16)


def _kvup(ckv, kr, W, transposed_v, first=0, count=None):
    _, sv, _ = ckv.shape
    bv = ckv.shape[0] if count is None else count
    tm = min(ROW_TILE, sv)
    hv = MLA_HEADS * V_DIM
    row_in = lambda w: pl.BlockSpec((1, tm, w), lambda b, j: (first + b, j, 0))
    row = lambda w: pl.BlockSpec((1, tm, w), lambda b, j: (b, j, 0))
    out_specs = [pl.BlockSpec((1, MLA_HEADS, tm, LANE), lambda b, j: (b, 0, j, 0))]
    out_shape = [jax.ShapeDtypeStruct((bv, MLA_HEADS, sv, LANE), bf16)]
    if transposed_v:
        out_specs += [pl.BlockSpec((1, MLA_HEADS, V_EXT, tm), lambda b, j: (b, 0, 0, j)),
                      pl.BlockSpec((1, 1, SUBLANE, LANE), lambda b, j: (b, j, 0, 0))]
        out_shape += [jax.ShapeDtypeStruct((bv, MLA_HEADS, V_EXT, sv), bf16),
                      jax.ShapeDtypeStruct((bv, sv // tm, SUBLANE, LANE), f32)]
        wv = W["w_vT"]
    else:
        out_specs += [row(hv)]
        out_shape += [jax.ShapeDtypeStruct((bv, sv, hv), bf16)]
        wv = W["w_v"]
    return pl.pallas_call(
        functools.partial(_kvup_kernel, transposed_v=transposed_v),
        grid=(bv, sv // tm),
        in_specs=[row_in(KV_RANK), row_in(LANE), _const_spec((KV_RANK, MLA_HEADS * LANE)), _const_spec(wv.shape),
                  _const_spec((1, LANE))],
        out_specs=out_specs,
        out_shape=out_shape,
        compiler_params=_params(("parallel", "parallel"), 32),
        name="kvup_t" if transposed_v else "kvup",
    )(ckv, kr, W["w_k"], wv, W["g_kn"])


def _mla_prompt_kernel(qi_ref, ki_ref, slow_ref, q_ref, k_ref, vt_ref, o_ref, m_sc, acc_sc, *, tile, sub):
    p = pl.program_id(1)
    qi, ki = qi_ref[p], ki_ref[p]
    slow = slow_ref[0] != 0
    nsub = tile // sub
    blk = min(ATT_BLOCK, tile)
    blk_key_chunk = lax.broadcasted_iota(jnp.int32, (blk, 1), 0) // CHUNK
    blk_query_chunk = lax.broadcasted_iota(jnp.int32, (1, blk), 1) // CHUNK
    query_chunk = lax.broadcasted_iota(jnp.int32, (1, sub), 1) // CHUNK

    @pl.when(ki == 0)
    def _():
        m_sc[...] = jnp.full(m_sc.shape, -jnp.inf, f32)
        acc_sc[...] = jnp.zeros(acc_sc.shape, f32)

    def scores(hd, lo, hi, cols):
        return _dot_nt(k_ref[0, hd, lo:hi, :], q_ref[0, hd, cols, :])

    def pv(hd, lo, hi, pm):
        return _dot(vt_ref[0, hd, :, lo:hi], pm.astype(bf16))

    def finish(hd, cols, acc):
        row0 = hd * V_DIM if isinstance(hd, int) else pl.multiple_of(hd * V_DIM, V_DIM)
        o_ref[0, pl.ds(row0, V_DIM), cols] = (acc[:V_DIM] / acc[V_DIM:V_DIM + 1]).astype(o_ref.dtype)

    def fast_step(hd, diag):
        for qs in range(tile // blk):
            cols = slice(qs * blk, (qs + 1) * blk)
            acc = acc_sc[hd, :, cols]
            for kb in range(qs if diag else tile // blk):
                lo = kb * blk
                acc = acc + pv(hd, lo, lo + blk, jnp.exp2(scores(hd, lo, lo + blk, cols)))
            if diag:
                lo = qs * blk
                pm = jnp.where(blk_key_chunk <= blk_query_chunk, jnp.exp2(scores(hd, lo, lo + blk, cols)), 0.0)
                finish(hd, cols, acc + pv(hd, lo, lo + blk, pm))
            else:
                acc_sc[hd, :, cols] = acc

    def slow_step(hd, diag):
        for qs in range(nsub):
            cols = slice(qs * sub, (qs + 1) * sub)
            kv_len = (qs + 1) * sub if diag else tile
            s = scores(hd, 0, kv_len, cols)
            if diag:
                kc = lax.broadcasted_iota(jnp.int32, (kv_len, 1), 0) // CHUNK
                s = jnp.where(kc <= query_chunk + (qs * sub) // CHUNK, s, -jnp.inf)
            m_old = m_sc[hd, :, cols]
            m_new = jnp.maximum(m_old, jnp.max(s, axis=0, keepdims=True))
            acc = jnp.exp2(m_old - m_new) * acc_sc[hd, :, cols] + pv(hd, 0, kv_len, jnp.exp2(s - m_new))
            if diag:
                finish(hd, cols, acc)
            else:
                m_sc[hd, :, cols] = m_new
                acc_sc[hd, :, cols] = acc

    for diag, where in ((False, ki != qi), (True, ki == qi)):
        @pl.when(jnp.logical_and(jnp.logical_not(slow), where))
        def _(diag=diag):
            lax.fori_loop(0, MLA_HEADS, lambda hd, c: (fast_step(hd, diag), c)[1], 0, unroll=2)

        @pl.when(jnp.logical_and(slow, where))
        def _(diag=diag):
            lax.fori_loop(0, MLA_HEADS, lambda hd, c: (slow_step(hd, diag), c)[1], 0)


def _mla_prompt(q, k, vt, slow):
    bv, _, sv, _ = q.shape
    tile = min(ATT_TILE, sv)
    sub = min(ATT_SUB, tile)
    nq = sv // tile
    pairs = [(i, j) for i in range(nq) for j in range(i + 1)]
    qi = jnp.asarray(np.array([a for a, _ in pairs], np.int32))
    ki = jnp.asarray(np.array([b for _, b in pairs], np.int32))
    hv = MLA_HEADS * V_DIM
    grid_spec = pltpu.PrefetchScalarGridSpec(
        num_scalar_prefetch=3,
        grid=(bv, len(pairs)),
        in_specs=[pl.BlockSpec((1, MLA_HEADS, tile, LANE), lambda b, p, qi, ki, sl: (b, 0, qi[p], 0)),
                  pl.BlockSpec((1, MLA_HEADS, tile, LANE), lambda b, p, qi, ki, sl: (b, 0, ki[p], 0)),
                  pl.BlockSpec((1, MLA_HEADS, V_EXT, tile), lambda b, p, qi, ki, sl: (b, 0, 0, ki[p]))],
        out_specs=pl.BlockSpec((1, hv, tile), lambda b, p, qi, ki, sl: (b, 0, qi[p])),
        scratch_shapes=[pltpu.VMEM((MLA_HEADS, 1, tile), f32), pltpu.VMEM((MLA_HEADS, V_EXT, tile), f32)],
    )
    return pl.pallas_call(
        functools.partial(_mla_prompt_kernel, tile=tile, sub=sub),
        grid_spec=grid_spec,
        out_shape=jax.ShapeDtypeStruct((bv, hv, sv), bf16),
        compiler_params=_params(("parallel", "arbitrary"), 56),
        name="mla_prompt",
    )(qi, ki, slow, q, k, vt)


def _mla_sample_kernel(q_ref, kp_ref, vp_ref, kn_ref, vn_ref, o_ref):
    vp = vp_ref[0]
    vn = vn_ref[0]
    lane_head = lax.broadcasted_iota(jnp.int32, (1, MLA_HEADS * V_DIM), 1) // V_DIM
    out = jnp.zeros(o_ref.shape[1:], f32)
    for hd in range(MLA_HEADS):
        qh = q_ref[0, hd]
        s1 = _dot_nt(qh, kp_ref[0, hd])
        s2 = _dot_nt(qh, kn_ref[0, hd])
        m = jnp.maximum(jnp.max(s1, axis=-1, keepdims=True), jnp.max(s2, axis=-1, keepdims=True))
        p1 = jnp.exp2(s1 - m)
        p2 = jnp.exp2(s2 - m)
        l = jnp.sum(p1, axis=-1, keepdims=True) + jnp.sum(p2, axis=-1, keepdims=True)
        sel = lane_head == hd
        o = _dot(p1.astype(bf16), jnp.where(sel, vp, 0)) + _dot(p2.astype(bf16), jnp.where(sel, vn, 0))
        out = out + o / l
    o_ref[0] = out.astype(o_ref.dtype)


def _mla_sample(q, kp, vp, kn, vn):
    bv, _, sv, _ = kn.shape
    past = kp.shape[2]
    hv = MLA_HEADS * V_DIM
    return pl.pallas_call(
        _mla_sample_kernel,
        grid=(bv,),
        in_specs=[pl.BlockSpec((1, MLA_HEADS, sv, LANE), lambda b: (0, 0, b, 0)),
                  pl.BlockSpec((1, MLA_HEADS, past, LANE), lambda b: (b, 0, 0, 0)),
                  pl.BlockSpec((1, past, hv), lambda b: (b, 0, 0)),
                  pl.BlockSpec((1, MLA_HEADS, sv, LANE), lambda b: (b, 0, 0, 0)),
                  pl.BlockSpec((1, sv, hv), lambda b: (b, 0, 0))],
        out_specs=pl.BlockSpec((1, sv, hv), lambda b: (b, 0, 0)),
        out_shape=jax.ShapeDtypeStruct((bv, sv, hv), bf16),
        compiler_params=_params(("parallel",), 32),
        name="mla_sample",
    )(q, kp, vp, kn, vn)


def _mlstm_kernel(mqk_ref, mv_ref, gate_ref, conv0_ref, c0_ref, n0_ref, m0_ref, wconv_ref, bconv_ref, bgate_ref,
                  h_ref, cout_ref, nout_ref, mout_ref, convout_ref, ext_sc, c_sc, n_sc, m_sc, *, chunk):
    step = pl.program_id(1)
    L = chunk

    @pl.when(step == 0)
    def _():
        ext_sc[0:SUBLANE, :] = conv0_ref[0]
        c_sc[...] = c0_ref[0]
        n_sc[...] = n0_ref[0]
        m_sc[...] = m0_ref[0]

    x = mqk_ref[0]
    ext_sc[SUBLANE:SUBLANE + L, :] = x
    y = bconv_ref[...] + x * wconv_ref[CONV_W - 1:CONV_W, :]
    for j in range(CONV_W - 1):
        off = SUBLANE - (CONV_W - 1) + j
        y = y + ext_sc[off:off + L, :] * wconv_ref[j:j + 1, :]
    ext_sc[0:SUBLANE, :] = x[L - SUBLANE:L, :]
    qk = y * _sigmoid(y)
    mq = qk[:, :ML_WIDTH]
    mk = qk[:, ML_WIDTH:] * (ML_DH ** -0.5)
    mv = mv_ref[0]

    g = gate_ref[0] + bgate_ref[...]
    ls = jnp.minimum(g, 0.0) - jnp.log1p(jnp.exp(-jnp.abs(g)))
    r_i = lax.broadcasted_iota(jnp.int32, (L, L), 0)
    c_i = lax.broadcasted_iota(jnp.int32, (L, L), 1)
    causal = c_i <= r_i
    tri = causal.astype(f32)
    bcum = jnp.dot(tri, ls, preferred_element_type=f32, precision=lax.Precision.HIGHEST)
    g_t = g.T[0:SUBLANE, :]
    bcum_t = lax.dot_general(ls.T[0:SUBLANE, :], tri, NT_DIMS, preferred_element_type=f32,
                             precision=lax.Precision.HIGHEST)

    lane = lax.broadcasted_iota(jnp.int32, (1, LANE), 1)
    sub8 = lax.broadcasted_iota(jnp.int32, (SUBLANE, 1), 0)
    last_row = lax.broadcasted_iota(jnp.int32, (L, 1), 0) == L - 1

    def col(t, idx):
        return jnp.sum(jnp.where(lane == idx, t, 0.0), axis=1, keepdims=True)

    def row(t, idx):
        return jnp.sum(jnp.where(sub8 == idx, t, 0.0), axis=0, keepdims=True)

    m_vec = m_sc[...]
    m_next = jnp.zeros_like(m_vec)
    for hd in range(ML_HEADS):
        hs = slice(hd * ML_DH, (hd + 1) * ML_DH)
        b_col, ig_col = col(bcum, GATE_F + hd), col(g, hd)
        b_row, ig_row = row(bcum_t, GATE_F + hd), row(g_t, hd)
        m_prev = col(m_vec, GATE_F + hd)
        inter = b_col + m_prev
        d = jnp.where(causal, b_col - b_row + ig_row, -jnp.inf)
        m_t = jnp.maximum(inter, jnp.max(d, axis=1, keepdims=True))
        w = jnp.exp(d - m_t)
        a_inter = jnp.exp(inter - m_t)
        qh, kh, vh = mq[:, hs], mk[:, hs], mv[:, hs]
        qb, vb = qh.astype(bf16), vh.astype(bf16)
        sqk = _dot_nt(qb, kh.astype(bf16)) * w
        c_h = c_sc[hd]
        n_h = n_sc[hd:hd + 1, :]
        num = a_inter * _dot(qb, c_h.astype(bf16)) + _dot(sqk.astype(bf16), vb)
        qn = a_inter * jnp.sum(qh * n_h, axis=1, keepdims=True) + jnp.sum(sqk, axis=1, keepdims=True)
        h_ref[0, :, hs] = num / jnp.maximum(jnp.abs(qn), jnp.exp(-m_t))
        b_last = jnp.sum(jnp.where(last_row, b_col, 0.0), axis=0, keepdims=True)
        m_end = jnp.sum(jnp.where(last_row, m_t, 0.0), axis=0, keepdims=True)
        decay = jnp.exp(b_last + m_prev - m_end)
        kw = kh * jnp.exp(b_last - b_col + ig_col - m_end)
        c_sc[hd] = decay * c_h + _dot(kw.T.astype(bf16), vb)
        n_sc[hd:hd + 1, :] = decay * n_h + jnp.sum(kw, axis=0, keepdims=True)
        m_next = m_next + jnp.where(lane == GATE_F + hd, m_end, 0.0)
    m_sc[...] = m_next

    @pl.when(step == pl.num_programs(1) - 1)
    def _():
        cout_ref[0] = c_sc[...]
        nout_ref[0] = n_sc[...]
        mout_ref[0] = m_sc[...]
        convout_ref[0] = ext_sc[0:SUBLANE, :]


def _mlstm(mqk, mv, gate, conv0, c0, n0, m0, W, chunk, c_first=0):
    bv, sv, _ = mqk.shape
    step = lambda w: pl.BlockSpec((1, chunk, w), lambda b, c: (b, c, 0))
    per_b = lambda *s: pl.BlockSpec((1,) + s, lambda b, c: (b,) + (0,) * len(s))
    c0_spec = pl.BlockSpec((1, ML_HEADS, ML_DH, ML_DH), lambda b, c: (c_first + b, 0, 0, 0))
    sds = jax.ShapeDtypeStruct
    return pl.pallas_call(
        functools.partial(_mlstm_kernel, chunk=chunk),
        grid=(bv, sv // chunk),
        in_specs=[step(2 * ML_WIDTH), step(ML_WIDTH), step(LANE),
                  per_b(SUBLANE, 2 * ML_WIDTH), c0_spec, per_b(ML_HEADS, ML_DH), per_b(1, LANE),
                  _const_spec((CONV_W, 2 * ML_WIDTH)), _const_spec((1, 2 * ML_WIDTH)), _const_spec((1, LANE))],
        out_specs=[step(ML_WIDTH), per_b(ML_HEADS, ML_DH, ML_DH), per_b(ML_HEADS, ML_DH), per_b(1, LANE),
                   per_b(SUBLANE, 2 * ML_WIDTH)],
        out_shape=[sds((bv, sv, ML_WIDTH), f32), sds((bv, ML_HEADS, ML_DH, ML_DH), f32), sds((bv, ML_HEADS, ML_DH), f32),
                   sds((bv, 1, LANE), f32), sds((bv, SUBLANE, 2 * ML_WIDTH), f32)],
        scratch_shapes=[pltpu.VMEM((SUBLANE + chunk, 2 * ML_WIDTH), f32), pltpu.VMEM((ML_HEADS, ML_DH, ML_DH), f32),
                        pltpu.VMEM((ML_HEADS, ML_DH), f32), pltpu.VMEM((1, LANE), f32)],
        compiler_params=_params(("parallel", "arbitrary"), 32),
        name="mlstm",
    )(mqk, mv, gate, conv0, c0, n0, m0, W["w_conv"], W["b_conv"], W["b_gate"])


def _outproj_rows(x, a, hm, og, gmh_ref, wa_ref, wm_ref):
    parts = []
    for hd in range(ML_HEADS):
        t = hm[:, hd * ML_DH:(hd + 1) * ML_DH]
        parts.append(t * lax.rsqrt(jnp.mean(t * t, axis=-1, keepdims=True) + EPS))
    hn = jnp.concatenate(parts, axis=1) * gmh_ref[...] * _sigmoid(og)
    return x + _dot(a, wa_ref[...]) + _dot(hn.astype(bf16), wm_ref[...])


def _outproj_kernel(x_ref, a_ref, hm_ref, og_ref, gmh_ref, wa_ref, wm_ref, o_ref):
    o_ref[0] = _outproj_rows(x_ref[0], a_ref[0], hm_ref[0], og_ref[0], gmh_ref, wa_ref, wm_ref)


def _outproj(x, a, hm, og, W):
    bv, sv, _ = x.shape
    tm = min(ROW_TILE, sv)
    hv = MLA_HEADS * V_DIM
    row = lambda w: pl.BlockSpec((1, tm, w), lambda b, j: (b, j, 0))
    return pl.pallas_call(
        _outproj_kernel,
        grid=(bv, sv // tm),
        in_specs=[row(D_MODEL), row(hv), row(ML_WIDTH), row(ML_WIDTH), _const_spec((1, ML_WIDTH)),
                  _const_spec((hv, D_MODEL)), _const_spec((ML_WIDTH, D_MODEL))],
        out_specs=row(D_MODEL),
        out_shape=jax.ShapeDtypeStruct(x.shape, f32),
        compiler_params=_params(("parallel", "parallel"), 32),
        name="outproj",
    )(x, a, hm, og, W["g_mh"], W["w_out_a"], W["w_out_m"])


def _xattn_rows(x, mk_heads, mv_heads, gx_ref, wq_ref, gxq_ref, wo_ref):
    qx = _dot(_rms(x, gx_ref[...]).astype(bf16), wq_ref[...])
    gxq = gxq_ref[...]
    outs = []
    for hd in range(XA_HEADS):
        t = _rms(qx[:, hd * XA_DH:(hd + 1) * XA_DH], gxq) * XA_SCALE
        s = _dot_nt(t.astype(bf16), mk_heads[hd])
        pm = jnp.exp(s - jnp.max(s, axis=-1, keepdims=True))
        outs.append(_dot(pm.astype(bf16), mv_heads[hd]) / jnp.sum(pm, axis=-1, keepdims=True))
    ox = jnp.concatenate(outs, axis=1)
    return x + _dot(ox.astype(bf16), wo_ref[...])


def _xattn_kernel(x_ref, mk_ref, mv_ref, gx_ref, wq_ref, gxq_ref, wo_ref, o_ref):
    mk_heads = [mk_ref[0, :, hd, :].astype(bf16) for hd in range(XA_HEADS)]
    mv_heads = [mv_ref[0, :, hd, :].astype(bf16) for hd in range(XA_HEADS)]
    o_ref[0] = _xattn_rows(x_ref[0], mk_heads, mv_heads, gx_ref, wq_ref, gxq_ref, wo_ref)


def _xattn(x, mem_k, mem_v, W, mem_first=0):
    bv, sv, _ = x.shape
    tm = min(ROW_TILE, sv)
    n_mem = mem_k.shape[1]
    row = pl.BlockSpec((1, tm, D_MODEL), lambda b, j: (b, j, 0))
    mem = pl.BlockSpec((1, n_mem, XA_HEADS, XA_DH), lambda b, j: (mem_first + b, 0, 0, 0))
    return pl.pallas_call(
        _xattn_kernel,
        grid=(bv, sv // tm),
        in_specs=[row, mem, mem, _const_spec((1, D_MODEL)), _const_spec((D_MODEL, D_MODEL)),
                  _const_spec((1, XA_DH)), _const_spec((D_MODEL, D_MODEL))],
        out_specs=row,
        out_shape=jax.ShapeDtypeStruct(x.shape, f32),
        compiler_params=_params(("parallel", "parallel"), 40),
        name="xattn",
    )(x, mem_k, mem_v, W["g_xattn"], W["w_xq"], W["g_xq"], W["w_xo"])


def _mlp_rows(x, g_ref, w1_ref, w2_ref):
    hf = _rms(x, g_ref[...]).astype(bf16)
    acc = x
    for c in range(D_FF // D_MODEL):
        cs = slice(c * D_MODEL, (c + 1) * D_MODEL)
        u = jnp.square(jnp.maximum(_dot(hf, w1_ref[:, cs]), 0.0))
        acc = acc + _dot(u.astype(bf16), w2_ref[cs, :])
    return acc


def _mlp_kernel(x_ref, g_ref, w1_ref, w2_ref, o_ref):
    o_ref[0] = _mlp_rows(x_ref[0], g_ref, w1_ref, w2_ref)


def _mlp(x, W):
    bv, sv, _ = x.shape
    tm = min(ROW_TILE, sv)
    row = pl.BlockSpec((1, tm, D_MODEL), lambda b, j: (b, j, 0))
    return pl.pallas_call(
        _mlp_kernel,
        grid=(bv, sv // tm),
        in_specs=[row, _const_spec((1, D_MODEL)), _const_spec((D_MODEL, D_FF)), _const_spec((D_FF, D_MODEL))],
        out_specs=row,
        out_shape=jax.ShapeDtypeStruct(x.shape, f32),
        compiler_params=_params(("parallel", "parallel"), 48),
        name="mlp",
    )(x, W["g_mlp"], W["w_ff1"], W["w_ff2"])


def _tail_kernel(x_ref, at_ref, hm_ref, og_ref, mk_ref, mv_ref, gmh_ref, wa_ref, wm_ref, gx_ref, wq_ref, gxq_ref,
                 wo_ref, gmlp_ref, w1_ref, w2_ref, o_ref, *, parts):
    mk, mv = mk_ref[0].astype(bf16), mv_ref[0].astype(bf16)
    mk_heads = [mk[:, hd * XA_DH:(hd + 1) * XA_DH] for hd in range(XA_HEADS)]
    mv_heads = [mv[:, hd * XA_DH:(hd + 1) * XA_DH] for hd in range(XA_HEADS)]
    rows = x_ref.shape[1] // parts
    for part in range(parts):
        rs = slice(part * rows, (part + 1) * rows)
        a = at_ref[0, :, rs].astype(f32).T.astype(bf16)
        x = _outproj_rows(x_ref[0, rs, :], a, hm_ref[0, rs, :], og_ref[0, rs, :], gmh_ref, wa_ref, wm_ref)
        x = _xattn_rows(x, mk_heads, mv_heads, gx_ref, wq_ref, gxq_ref, wo_ref)
        o_ref[0, rs, :] = _mlp_rows(x, gmlp_ref, w1_ref, w2_ref)


def _tail(x, at, hm, og, mem_k, mem_v, W):
    bv, sv, _ = x.shape
    tm = min(ROW_TILE, sv)
    hv = MLA_HEADS * V_DIM
    n_mem = mem_k.shape[1]
    row = lambda w: pl.BlockSpec((1, tm, w), lambda b, j: (b, j, 0))
    mem = pl.BlockSpec((1, n_mem, D_MODEL), lambda b, j: (b, 0, 0))
    return pl.pallas_call(
        functools.partial(_tail_kernel, parts=2 if tm % (2 * LANE) == 0 else 1),
        grid=(bv, sv // tm),
        in_specs=[row(D_MODEL), pl.BlockSpec((1, hv, tm), lambda b, j: (b, 0, j)), row(ML_WIDTH), row(ML_WIDTH), mem, mem,
                  _const_spec((1, ML_WIDTH)), _const_spec((hv, D_MODEL)), _const_spec((ML_WIDTH, D_MODEL)),
                  _const_spec((1, D_MODEL)), _const_spec((D_MODEL, D_MODEL)), _const_spec((1, XA_DH)),
                  _const_spec((D_MODEL, D_MODEL)),
                  _const_spec((1, D_MODEL)), _const_spec((D_MODEL, D_FF)), _const_spec((D_FF, D_MODEL))],
        out_specs=row(D_MODEL),
        out_shape=jax.ShapeDtypeStruct(x.shape, f32),
        compiler_params=_params(("parallel", "parallel"), 58),
        name="tail",
    )(x, at, hm, og, mem_k, mem_v, W["g_mh"], W["w_out_a"], W["w_out_m"], W["g_xattn"], W["w_xq"], W["g_xq"], W["w_xo"],
      W["g_mlp"], W["w_ff1"], W["w_ff2"])


def _memkv_kernel(mem_ref, gm_ref, wk_ref, wv_ref, gk_ref, k_ref, v_ref):
    hm = _rms(mem_ref[0], gm_ref[...]).astype(bf16)
    k = _dot(hm, wk_ref[...])
    gk = gk_ref[...]
    k_ref[0] = jnp.concatenate([_rms(k[:, hd * XA_DH:(hd + 1) * XA_DH], gk) for hd in range(XA_HEADS)], axis=1)
    v_ref[0] = _dot(hm, wv_ref[...])


def _memkv(mem, W):
    bv, n_mem, _ = mem.shape
    blk = pl.BlockSpec((1, n_mem, D_MODEL), lambda b: (b, 0, 0))
    return pl.pallas_call(
        _memkv_kernel,
        grid=(bv,),
        in_specs=[blk, _const_spec((1, D_MODEL)), _const_spec((D_MODEL, D_MODEL)), _const_spec((D_MODEL, D_MODEL)),
                  _const_spec((1, XA_DH))],
        out_specs=[blk, blk],
        out_shape=[jax.ShapeDtypeStruct(mem.shape, f32)] * 2,
        compiler_params=_params(("parallel",), 32),
        name="memkv",
    )(mem, W["g_mem"], W["w_xk"], W["w_xv"], W["g_xk"])


def _pad_lanes(a, lo, width=LANE):
    pad = [(0, 0)] * (a.ndim - 1) + [(lo, width - lo - a.shape[-1])]
    return jnp.pad(a, pad)


def _rope_tables(pos):
    half = ROPE_DIM // 2
    inv_freq = ROPE_THETA ** (-jnp.arange(half, dtype=f32) / half)
    ang = pos.astype(f32)[:, None] * inv_freq[None, :]
    cos, sin = jnp.cos(ang), jnp.sin(ang)
    n = pos.shape[0]
    ones, zeros = jnp.ones((n, NOPE_DIM), f32), jnp.zeros((n, half), f32)
    tail = jnp.zeros((n, LANE - QK_DIM), f32)
    cos_t = jnp.concatenate([ones, cos, cos, tail], axis=1)
    sin_a = jnp.concatenate([0 * ones, -sin, zeros, tail], axis=1)
    sin_b = jnp.concatenate([0 * ones, zeros, sin, tail], axis=1)
    return cos_t, sin_a, sin_b


def _layer_weights(l, g_mix, w_in, g_qa, w_q_up, g_qnorm, g_kva, w_kv_up, g_knorm, w_conv, b_conv, b_igate, b_fgate,
                   g_mhead, w_out, g_xattn, g_mem, w_xq, w_xk, w_xv, g_xq, g_xk, w_xo, g_mlp, w_ff1, w_ff2):
    wi = w_in[l]
    off_kr = Q_RANK + KV_RANK
    off_mqk = off_kr + ROPE_DIM
    off_mv = off_mqk + 2 * ML_WIDTH
    off_mi = off_mv + ML_WIDTH
    off_mo = off_mi + 2 * ML_HEADS
    packed = jnp.concatenate([
        wi[:, :off_kr],
        _pad_lanes(wi[:, off_kr:off_mqk], NOPE_DIM),
        wi[:, off_mqk:off_mi],
        wi[:, off_mo:],
        _pad_lanes(wi[:, off_mi:off_mo], 0),
    ], axis=1)
    wq = w_q_up[l].reshape(Q_RANK, MLA_HEADS, QK_DIM)
    wkv = w_kv_up[l].reshape(KV_RANK, MLA_HEADS, NOPE_DIM + V_DIM)
    wv = wkv[:, :, NOPE_DIM:].reshape(KV_RANK, MLA_HEADS * V_DIM)
    row = lambda a: a.reshape(1, -1)
    return {
        "g_mix": row(g_mix[l]), "w_in": packed.astype(bf16), "g_qa": row(g_qa[l]),
        "w_q": _pad_lanes(wq, 0).reshape(Q_RANK, MLA_HEADS * LANE).astype(bf16),
        "g_qn": _pad_lanes(row(g_qnorm[l]), 0), "g_kva": row(g_kva[l]),
        "w_k": _pad_lanes(wkv[:, :, :NOPE_DIM], 0).reshape(KV_RANK, MLA_HEADS * LANE).astype(bf16),
        "w_v": wv.astype(bf16), "w_vT": wv.T.astype(bf16),
        "g_kn": _pad_lanes(row(g_knorm[l]), 0),
        "w_conv": w_conv[l], "b_conv": row(b_conv[l]),
        "b_gate": _pad_lanes(row(jnp.concatenate([b_igate[l], b_fgate[l]])), 0),
        "g_mh": row(g_mhead[l]),
        "w_out_a": w_out[l][:MLA_HEADS * V_DIM].astype(bf16), "w_out_m": w_out[l][MLA_HEADS * V_DIM:].astype(bf16),
        "g_xattn": row(g_xattn[l]), "g_mem": row(g_mem[l]),
        "w_xq": w_xq[l].astype(bf16), "w_xk": w_xk[l].astype(bf16), "w_xv": w_xv[l].astype(bf16),
        "g_xq": row(g_xq[l]), "g_xk": row(g_xk[l]), "w_xo": w_xo[l].astype(bf16),
        "g_mlp": row(g_mlp[l]), "w_ff1": w_ff1[l].astype(bf16), "w_ff2": w_ff2[l].astype(bf16),
    }


def _unpack_state(c_new, n_new, m_new, conv_new):
    return (c_new, n_new, m_new[:, 0, GATE_F:GATE_F + ML_HEADS], conv_new[:, SUBLANE - (CONV_W - 1):, :])


def kernel(x_prompt, x_sample, cache_mla_ckv, cache_mla_krope, state_mlstm_C, state_mlstm_n, state_mlstm_m,
           state_mlstm_conv, cache_mem_k, cache_mem_v, mem_prompt, g_mix, w_in, g_qa, w_q_up, g_qnorm, g_kva,
           w_kv_up, g_knorm, w_conv, b_conv, b_igate, b_fgate, g_mhead, w_out, g_xattn, g_mem, w_xq, w_xk, w_xv,
           g_xq, g_xk, w_xo, g_mlp, w_ff1, w_ff2):
    depth = w_in.shape[0]
    bp, sp, _ = x_prompt.shape
    bs, ss, _ = x_sample.shape
    past = cache_mla_ckv.shape[2]
    n_mem = mem_prompt.shape[1]
    weights = (g_mix, w_in, g_qa, w_q_up, g_qnorm, g_kva, w_kv_up, g_knorm, w_conv, b_conv, b_igate, b_fgate,
               g_mhead, w_out, g_xattn, g_mem, w_xq, w_xk, w_xv, g_xq, g_xk, w_xo, g_mlp, w_ff1, w_ff2)

    tabs_p = _rope_tables(jnp.arange(sp))
    tabs_s = tuple(jnp.tile(t, (bs, 1)) for t in _rope_tables(past + jnp.arange(ss)))
    ml_chunk = min(ML_CHUNK, sp)
    cache_ckv = cache_mla_ckv.reshape(depth * bs, past, KV_RANK)
    cache_kr = _pad_lanes(cache_mla_krope, NOPE_DIM).reshape(depth * bs, past, LANE)
    cache_c = state_mlstm_C.reshape(depth * bs, ML_HEADS, ML_DH, ML_DH)
    cache_mk = cache_mem_k.reshape(depth * bs, n_mem, XA_HEADS, XA_DH)
    cache_mv = cache_mem_v.reshape(depth * bs, n_mem, XA_HEADS, XA_DH)

    xp, xs = x_prompt, x_sample.reshape(1, bs * ss, D_MODEL)
    p_out = [[] for _ in range(8)]
    s_out = [[] for _ in range(6)]
    for l in range(depth):
        W = _layer_weights(l, *weights)

        mem_k, mem_v = _memkv(mem_prompt, W)
        q, ckv, kr, mqk, mv, og, gate, qn2 = _inproj(xp, tabs_p, W)
        k, vt, kn2 = _kvup(ckv, kr, W, True)
        safe = jnp.max(qn2) * jnp.max(kn2) * 1.05 <= SAFE_LOG2_SCORE ** 2
        a = _mla_prompt(q, k, vt, jnp.logical_not(safe).astype(jnp.int32).reshape(1))
        zeros = lambda *s: jnp.zeros((bp,) + s, f32)
        hm, *state = _mlstm(mqk, mv, gate, zeros(SUBLANE, 2 * ML_WIDTH), zeros(ML_HEADS, ML_DH, ML_DH),
                            zeros(ML_HEADS, ML_DH), zeros(1, LANE), W, ml_chunk)
        xp = _tail(xp, a, hm, og, mem_k, mem_v, W)
        new = (ckv, kr[..., NOPE_DIM:QK_DIM]) + _unpack_state(*state) + (
            mem_k.reshape(bp, n_mem, XA_HEADS, XA_DH), mem_v.reshape(bp, n_mem, XA_HEADS, XA_DH))
        for lst, t in zip(p_out, new):
            lst.append(t)

        q, ckv, kr, mqk, mv, og, gate, _ = _inproj(xs, tabs_s, W)
        per_stream = lambda t: t.reshape(bs, ss, t.shape[-1])
        ckv, kr = per_stream(ckv), per_stream(kr)
        kn, vn = _kvup(ckv, kr, W, False)
        kp, vp = _kvup(cache_ckv, cache_kr, W, False, first=l * bs, count=bs)
        a = _mla_sample(q, kp, vp, kn, vn)
        conv0 = jnp.pad(state_mlstm_conv[l], ((0, 0), (SUBLANE - (CONV_W - 1), 0), (0, 0)))
        m0 = _pad_lanes(state_mlstm_m[l], GATE_F).reshape(bs, 1, LANE)
        hm, *state = _mlstm(per_stream(mqk), per_stream(mv), per_stream(gate), conv0, cache_c,
                            state_mlstm_n[l], m0, W, ss, c_first=l * bs)
        flat = lambda t: t.reshape(1, bs * ss, t.shape[-1])
        xs = _outproj(xs, flat(a), flat(hm), og, W)
        xs = flat(_xattn(per_stream(xs), cache_mk, cache_mv, W, mem_first=l * bs))
        xs = _mlp(xs, W)
        new = (ckv, kr[..., NOPE_DIM:QK_DIM]) + _unpack_state(*state)
        for lst, t in zip(s_out, new):
            lst.append(t)

    outs_p = tuple(jnp.stack(t) for t in p_out)
    outs_s = tuple(jnp.stack(t) for t in s_out)
    return (xp, xs.reshape(bs, ss, D_MODEL)) + outs_p + outs_s
```

```python
import functools

import jax
import jax.numpy as jnp
import numpy as np
from jax import lax
from jax.experimental import pallas as pl
from jax.experimental.pallas import tpu as pltpu

f32 = jnp.float32
bf16 = jnp.bfloat16

D_MODEL = 1024
CHUNK = 64
EPS = 1e-6
MLA_HEADS = 8
Q_RANK = 256
KV_RANK = 128
NOPE_DIM = 64
ROPE_DIM = 32
QK_DIM = NOPE_DIM + ROPE_DIM
V_DIM = 64
ROPE_THETA = 10000.0
MLA_SCALE = QK_DIM ** -0.5
ML_HEADS = 4
ML_DH = 128
ML_WIDTH = ML_HEADS * ML_DH
CONV_W = 4
XA_HEADS = 4
XA_DH = D_MODEL // XA_HEADS
XA_SCALE = XA_DH ** -0.5
D_FF = 4 * D_MODEL

LANE = 128
SUBLANE = 8
MIB = 1024 * 1024

C_QA = 0
C_KVA = C_QA + Q_RANK
C_KR = C_KVA + KV_RANK
C_MQK = C_KR + LANE
C_MV = C_MQK + 2 * ML_WIDTH
C_MO = C_MV + ML_WIDTH
C_GATE = C_MO + ML_WIDTH
IN_COLS_PACKED = C_GATE + LANE
GATE_F = ML_HEADS

ROW_TILE = 512
ATT_TILE = 2048
ATT_BLOCK = 1024
ATT_SUB = 512
FAST_HEADS_PER_TRIP = 4
ML_CHUNK = 256

BF16_ROWS = 16
V_EXT = V_DIM + BF16_ROWS
LOG2E = 1.4426950408889634
SAFE_LOG2_SCORE = 64.0

NT_DIMS = (((1,), (1,)), ((), ()))


def _rms(x, g):
    return x * lax.rsqrt(jnp.mean(x * x, axis=-1, keepdims=True) + EPS) * g


def _dot(a, b):
    return jnp.dot(a, b, preferred_element_type=f32)


def _dot_nt(a, b):
    return lax.dot_general(a, b, NT_DIMS, preferred_element_type=f32)


def _sigmoid(x):
    return 1.0 / (1.0 + jnp.exp(-x))


def _const_spec(shape):
    nd = len(shape)
    return pl.BlockSpec(shape, lambda *_: (0,) * nd, pipeline_mode=pl.Buffered(1))


def _params(semantics, vmem_mib):
    return pltpu.CompilerParams(dimension_semantics=semantics, vmem_limit_bytes=vmem_mib * MIB)


def _inproj_kernel(x_ref, cos_ref, sa_ref, sb_ref, gmix_ref, win_ref, gqa_ref, wq_ref, gqn_ref, gkva_ref,
                   q_ref, ckv_ref, kr_ref, mqk_ref, mv_ref, og_ref, gate_ref):
    h = _rms(x_ref[0], gmix_ref[...]).astype(bf16)
    proj = lambda lo, hi: _dot(h, win_ref[:, lo:hi])
    cos, sa, sb = cos_ref[...], sa_ref[...], sb_ref[...]

    def rope(t):
        return t * cos + pltpu.roll(t, LANE - ROPE_DIM // 2, 1) * sa + pltpu.roll(t, ROPE_DIM // 2, 1) * sb

    q_lat = _rms(proj(C_QA, C_KVA), gqa_ref[...])
    qf = _dot(q_lat.astype(bf16), wq_ref[...])
    gq = gqn_ref[...] * (MLA_SCALE * LOG2E)
    for hd in range(MLA_HEADS):
        t = rope(qf[:, hd * LANE:(hd + 1) * LANE])
        ss = jnp.sum(t * t, axis=-1, keepdims=True) * (1.0 / QK_DIM)
        q_ref[0, hd] = (t * lax.rsqrt(ss + EPS) * gq).astype(bf16)
    latent = proj(C_KVA, C_MQK)
    ckv_ref[0] = _rms(latent[:, :KV_RANK], gkva_ref[...])
    kr_ref[0] = rope(latent[:, KV_RANK:])
    mqk_ref[0] = proj(C_MQK, C_MV)
    mv_ref[0] = proj(C_MV, C_MO)
    og_ref[0] = proj(C_MO, C_GATE)
    gate_ref[0] = proj(C_GATE, IN_COLS_PACKED)


def _inproj(x, tabs, W):
    bv, sv, _ = x.shape
    tm = min(ROW_TILE, sv)
    row = lambda w: pl.BlockSpec((1, tm, w), lambda b, j: (b, j, 0))
    tab = pl.BlockSpec((tm, LANE), lambda b, j: (j, 0))
    sds = jax.ShapeDtypeStruct
    return pl.pallas_call(
        _inproj_kernel,
        grid=(bv, sv // tm),
        in_specs=[row(D_MODEL), tab, tab, tab,
                  _const_spec((1, D_MODEL)), _const_spec((D_MODEL, IN_COLS_PACKED)),
                  _const_spec((1, Q_RANK)), _const_spec((Q_RANK, MLA_HEADS * LANE)),
                  _const_spec((1, LANE)), _const_spec((1, KV_RANK))],
        out_specs=[pl.BlockSpec((1, MLA_HEADS, tm, LANE), lambda b, j: (b, 0, j, 0)),
                   row(KV_RANK), row(LANE), row(2 * ML_WIDTH), row(ML_WIDTH), row(ML_WIDTH), row(LANE)],
        out_shape=[sds((bv, MLA_HEADS, sv, LANE), bf16), sds((bv, sv, KV_RANK), f32), sds((bv, sv, LANE), f32),
                   sds((bv, sv, 2 * ML_WIDTH), f32), sds((bv, sv, ML_WIDTH), f32), sds((bv, sv, ML_WIDTH), f32),
                   sds((bv, sv, LANE), f32)],
        compiler_params=_params(("parallel", "parallel"), 48),
        name="inproj",
    )(x, *tabs, W["g_mix"], W["w_in"], W["g_qa"], W["w_q"], W["g_qn"], W["g_kva"])


def _kvup_kernel(ckv_ref, kr_ref, wk_ref, wv_ref, gk_ref, k_ref, v_ref, *, transposed_v):
    c = ckv_ref[0].astype(bf16)
    kn = _dot(c, wk_ref[...])
    kr = kr_ref[0]
    g = gk_ref[...]
    for hd in range(MLA_HEADS):
        t = kn[:, hd * LANE:(hd + 1) * LANE] + kr
        ss = jnp.sum(t * t, axis=-1, keepdims=True) * (1.0 / QK_DIM)
        k_ref[0, hd] = (t * lax.rsqrt(ss + EPS) * g).astype(bf16)
    if transposed_v:
        vt = _dot_nt(wv_ref[...], c).astype(bf16)
        tm = vt.shape[1]
        ones_tile = (lax.broadcasted_iota(jnp.int32, (BF16_ROWS, tm), 0) == 0).astype(bf16)
        for hd in range(MLA_HEADS):
            v_ref[0, hd, 0:V_DIM, :] = vt[hd * V_DIM:(hd + 1) * V_DIM, :]
            v_ref[0, hd, V_DIM:V_EXT, :] = ones_tile
    else:
        v_ref[0] = _dot(c, wv_ref[...]).astype(bf16)


def _kvup(ckv, kr, W, transposed_v, first=0, count=None):
    _, sv, _ = ckv.shape
    bv = ckv.shape[0] if count is None else count
    tm = min(ROW_TILE, sv)
    hv = MLA_HEADS * V_DIM
    row_in = lambda w: pl.BlockSpec((1, tm, w), lambda b, j: (first + b, j, 0))
    row = lambda w: pl.BlockSpec((1, tm, w), lambda b, j: (b, j, 0))
    out_specs = [pl.BlockSpec((1, MLA_HEADS, tm, LANE), lambda b, j: (b, 0, j, 0))]
    out_shape = [jax.ShapeDtypeStruct((bv, MLA_HEADS, sv, LANE), bf16)]
    if transposed_v:
        out_specs += [pl.BlockSpec((1, MLA_HEADS, V_EXT, tm), lambda b, j: (b, 0, 0, j))]
        out_shape += [jax.ShapeDtypeStruct((bv, MLA_HEADS, V_EXT, sv), bf16)]
        wv = W["w_vT"]
    else:
        out_specs += [row(hv)]
        out_shape += [jax.ShapeDtypeStruct((bv, sv, hv), bf16)]
        wv = W["w_v"]
    return pl.pallas_call(
        functools.partial(_kvup_kernel, transposed_v=transposed_v),
        grid=(bv, sv // tm),
        in_specs=[row_in(KV_RANK), row_in(LANE), _const_spec((KV_RANK, MLA_HEADS * LANE)), _const_spec(wv.shape),
                  _const_spec((1, LANE))],
        out_specs=out_specs,
        out_shape=out_shape,
        compiler_params=_params(("parallel", "parallel"), 32),
        name="kvup_t" if transposed_v else "kvup",
    )(ckv, kr, W["w_k"], wv, W["g_kn"])


def _mla_prompt_kernel(qi_ref, ki_ref, q_ref, k_ref, vt_ref, o_ref, m_sc, acc_sc, *, tile, sub, bounded):
    p = pl.program_id(1)
    qi, ki = qi_ref[p], ki_ref[p]
    nsub = tile // sub
    blk = min(ATT_BLOCK, tile)
    blk_key_chunk = lax.broadcasted_iota(jnp.int32, (blk, 1), 0) // CHUNK
    blk_query_chunk = lax.broadcasted_iota(jnp.int32, (1, blk), 1) // CHUNK
    query_chunk = lax.broadcasted_iota(jnp.int32, (1, sub), 1) // CHUNK

    @pl.when(ki == 0)
    def _():
        m_sc[...] = jnp.full(m_sc.shape, -jnp.inf, f32)
        acc_sc[...] = jnp.zeros(acc_sc.shape, f32)

    def scores(hd, lo, hi, cols):
        return _dot_nt(k_ref[0, hd, lo:hi, :], q_ref[0, hd, cols, :])

    def pv(hd, lo, hi, pm):
        return _dot(vt_ref[0, hd, :, lo:hi], pm.astype(bf16))

    def finish(hd, cols, acc):
        row0 = hd * V_DIM if isinstance(hd, int) else pl.multiple_of(hd * V_DIM, V_DIM)
        o_ref[0, pl.ds(row0, V_DIM), cols] = (acc[:V_DIM] / acc[V_DIM:V_DIM + 1]).astype(o_ref.dtype)

    def fast_step(hd, diag):
        for qs in range(tile // blk):
            cols = slice(qs * blk, (qs + 1) * blk)
            acc = acc_sc[hd, :, cols]
            for kb in range(qs if diag else tile // blk):
                lo = kb * blk
                acc = acc + pv(hd, lo, lo + blk, jnp.exp2(scores(hd, lo, lo + blk, cols)))
            if diag:
                lo = qs * blk
                pm = jnp.where(blk_key_chunk <= blk_query_chunk, jnp.exp2(scores(hd, lo, lo + blk, cols)), 0.0)
                finish(hd, cols, acc + pv(hd, lo, lo + blk, pm))
            else:
                acc_sc[hd, :, cols] = acc

    def slow_step(hd, diag):
        for qs in range(nsub):
            cols = slice(qs * sub, (qs + 1) * sub)
            kv_len = (qs + 1) * sub if diag else tile
            s = scores(hd, 0, kv_len, cols)
            if diag:
                kc = lax.broadcasted_iota(jnp.int32, (kv_len, 1), 0) // CHUNK
                s = jnp.where(kc <= query_chunk + (qs * sub) // CHUNK, s, -jnp.inf)
            m_old = m_sc[hd, :, cols]
            m_new = jnp.maximum(m_old, jnp.max(s, axis=0, keepdims=True))
            acc = jnp.exp2(m_old - m_new) * acc_sc[hd, :, cols] + pv(hd, 0, kv_len, jnp.exp2(s - m_new))
            if diag:
                finish(hd, cols, acc)
            else:
                m_sc[hd, :, cols] = m_new
                acc_sc[hd, :, cols] = acc

    for diag, where in ((False, ki != qi), (True, ki == qi)):
        @pl.when(where)
        def _(diag=diag):
            if bounded:
                lax.fori_loop(0, MLA_HEADS, lambda hd, c: (fast_step(hd, diag), c)[1], 0, unroll=FAST_HEADS_PER_TRIP)
            else:
                lax.fori_loop(0, MLA_HEADS, lambda hd, c: (slow_step(hd, diag), c)[1], 0)


def _mla_prompt(q, k, vt, bounded):
    bv, _, sv, _ = q.shape
    tile = min(ATT_TILE, sv)
    sub = min(ATT_SUB, tile)
    nq = sv // tile
    pairs = [(i, j) for i in range(nq) for j in range(i + 1)]
    qi = jnp.asarray(np.array([a for a, _ in pairs], np.int32))
    ki = jnp.asarray(np.array([b for _, b in pairs], np.int32))
    hv = MLA_HEADS * V_DIM
    grid_spec = pltpu.PrefetchScalarGridSpec(
        num_scalar_prefetch=2,
        grid=(bv, len(pairs)),
        in_specs=[pl.BlockSpec((1, MLA_HEADS, tile, LANE), lambda b, p, qi, ki: (b, 0, qi[p], 0)),
                  pl.BlockSpec((1, MLA_HEADS, tile, LANE), lambda b, p, qi, ki: (b, 0, ki[p], 0)),
                  pl.BlockSpec((1, MLA_HEADS, V_EXT, tile), lambda b, p, qi, ki: (b, 0, 0, ki[p]))],
        out_specs=pl.BlockSpec((1, hv, tile), lambda b, p, qi, ki: (b, 0, qi[p])),
        scratch_shapes=[pltpu.VMEM((MLA_HEADS, 1, tile), f32), pltpu.VMEM((MLA_HEADS, V_EXT, tile), f32)],
    )
    return pl.pallas_call(
        functools.partial(_mla_prompt_kernel, tile=tile, sub=sub, bounded=bounded),
        grid_spec=grid_spec,
        out_shape=jax.ShapeDtypeStruct((bv, hv, sv), bf16),
        compiler_params=_params(("parallel", "arbitrary"), 56),
        name="mla_prompt" if bounded else "mla_prompt_running_max",
    )(qi, ki, q, k, vt)


def _mla_sample_kernel(q_ref, kp_ref, vp_ref, kn_ref, vn_ref, o_ref):
    vp = vp_ref[0]
    vn = vn_ref[0]
    lane_head = lax.broadcasted_iota(jnp.int32, (1, MLA_HEADS * V_DIM), 1) // V_DIM
    out = jnp.zeros(o_ref.shape[1:], f32)
    for hd in range(MLA_HEADS):
        qh = q_ref[0, hd]
        s1 = _dot_nt(qh, kp_ref[0, hd])
        s2 = _dot_nt(qh, kn_ref[0, hd])
        m = jnp.maximum(jnp.max(s1, axis=-1, keepdims=True), jnp.max(s2, axis=-1, keepdims=True))
        p1 = jnp.exp2(s1 - m)
        p2 = jnp.exp2(s2 - m)
        l = jnp.sum(p1, axis=-1, keepdims=True) + jnp.sum(p2, axis=-1, keepdims=True)
        sel = lane_head == hd
        o = _dot(p1.astype(bf16), jnp.where(sel, vp, 0)) + _dot(p2.astype(bf16), jnp.where(sel, vn, 0))
        out = out + o / l
    o_ref[0] = out.astype(o_ref.dtype)


def _mla_sample(q, kp, vp, kn, vn):
    bv, _, sv, _ = kn.shape
    past = kp.shape[2]
    hv = MLA_HEADS * V_DIM
    return pl.pallas_call(
        _mla_sample_kernel,
        grid=(bv,),
        in_specs=[pl.BlockSpec((1, MLA_HEADS, sv, LANE), lambda b: (0, 0, b, 0)),
                  pl.BlockSpec((1, MLA_HEADS, past, LANE), lambda b: (b, 0, 0, 0)),
                  pl.BlockSpec((1, past, hv), lambda b: (b, 0, 0)),
                  pl.BlockSpec((1, MLA_HEADS, sv, LANE), lambda b: (b, 0, 0, 0)),
                  pl.BlockSpec((1, sv, hv), lambda b: (b, 0, 0))],
        out_specs=pl.BlockSpec((1, sv, hv), lambda b: (b, 0, 0)),
        out_shape=jax.ShapeDtypeStruct((bv, sv, hv), bf16),
        compiler_params=_params(("parallel",), 32),
        name="mla_sample",
    )(q, kp, vp, kn, vn)


def _mlstm_kernel(mqk_ref, mv_ref, gate_ref, conv0_ref, c0_ref, n0_ref, m0_ref, wconv_ref, bconv_ref, bgate_ref,
                  h_ref, cout_ref, nout_ref, mout_ref, convout_ref, ext_sc, c_sc, n_sc, m_sc, *, chunk):
    step = pl.program_id(1)
    L = chunk

    @pl.when(step == 0)
    def _():
        ext_sc[0:SUBLANE, :] = conv0_ref[0]
        c_sc[...] = c0_ref[0]
        n_sc[...] = n0_ref[0]
        m_sc[...] = m0_ref[0]

    x = mqk_ref[0]
    ext_sc[SUBLANE:SUBLANE + L, :] = x
    y = bconv_ref[...] + x * wconv_ref[CONV_W - 1:CONV_W, :]
    for j in range(CONV_W - 1):
        off = SUBLANE - (CONV_W - 1) + j
        y = y + ext_sc[off:off + L, :] * wconv_ref[j:j + 1, :]
    ext_sc[0:SUBLANE, :] = x[L - SUBLANE:L, :]
    qk = y * _sigmoid(y)
    mq = qk[:, :ML_WIDTH]
    mk = qk[:, ML_WIDTH:] * (ML_DH ** -0.5)
    mv = mv_ref[0]

    g = gate_ref[0] + bgate_ref[...]
    ls = jnp.minimum(g, 0.0) - jnp.log1p(jnp.exp(-jnp.abs(g)))
    r_i = lax.broadcasted_iota(jnp.int32, (L, L), 0)
    c_i = lax.broadcasted_iota(jnp.int32, (L, L), 1)
    causal = c_i <= r_i
    tri = causal.astype(f32)
    bcum = jnp.dot(tri, ls, preferred_element_type=f32, precision=lax.Precision.HIGHEST)
    g_t = g.T[0:SUBLANE, :]
    bcum_t = lax.dot_general(ls.T[0:SUBLANE, :], tri, NT_DIMS, preferred_element_type=f32,
                             precision=lax.Precision.HIGHEST)

    lane = lax.broadcasted_iota(jnp.int32, (1, LANE), 1)
    sub8 = lax.broadcasted_iota(jnp.int32, (SUBLANE, 1), 0)
    last_row = lax.broadcasted_iota(jnp.int32, (L, 1), 0) == L - 1

    def col(t, idx):
        return jnp.sum(jnp.where(lane == idx, t, 0.0), axis=1, keepdims=True)

    def row(t, idx):
        return jnp.sum(jnp.where(sub8 == idx, t, 0.0), axis=0, keepdims=True)

    m_vec = m_sc[...]
    m_next = jnp.zeros_like(m_vec)
    for hd in range(ML_HEADS):
        hs = slice(hd * ML_DH, (hd + 1) * ML_DH)
        b_col, ig_col = col(bcum, GATE_F + hd), col(g, hd)
        b_row, ig_row = row(bcum_t, GATE_F + hd), row(g_t, hd)
        m_prev = col(m_vec, GATE_F + hd)
        inter = b_col + m_prev
        d = jnp.where(causal, b_col - b_row + ig_row, -jnp.inf)
        m_t = jnp.maximum(inter, jnp.max(d, axis=1, keepdims=True))
        w = jnp.exp(d - m_t)
        a_inter = jnp.exp(inter - m_t)
        qh, kh, vh = mq[:, hs], mk[:, hs], mv[:, hs]
        qb, vb = qh.astype(bf16), vh.astype(bf16)
        sqk = _dot_nt(qb, kh.astype(bf16)) * w
        c_h = c_sc[hd]
        n_h = n_sc[hd:hd + 1, :]
        num = a_inter * _dot(qb, c_h.astype(bf16)) + _dot(sqk.astype(bf16), vb)
        qn = a_inter * jnp.sum(qh * n_h, axis=1, keepdims=True) + jnp.sum(sqk, axis=1, keepdims=True)
        h_ref[0, :, hs] = num / jnp.maximum(jnp.abs(qn), jnp.exp(-m_t))
        b_last = jnp.sum(jnp.where(last_row, b_col, 0.0), axis=0, keepdims=True)
        m_end = jnp.sum(jnp.where(last_row, m_t, 0.0), axis=0, keepdims=True)
        decay = jnp.exp(b_last + m_prev - m_end)
        kw = kh * jnp.exp(b_last - b_col + ig_col - m_end)
        c_sc[hd] = decay * c_h + _dot(kw.T.astype(bf16), vb)
        n_sc[hd:hd + 1, :] = decay * n_h + jnp.sum(kw, axis=0, keepdims=True)
        m_next = m_next + jnp.where(lane == GATE_F + hd, m_end, 0.0)
    m_sc[...] = m_next

    @pl.when(step == pl.num_programs(1) - 1)
    def _():
        cout_ref[0] = c_sc[...]
        nout_ref[0] = n_sc[...]
        mout_ref[0] = m_sc[...]
        convout_ref[0] = ext_sc[0:SUBLANE, :]


def _mlstm(mqk, mv, gate, conv0, c0, n0, m0, W, chunk, c_first=0):
    bv, sv, _ = mqk.shape
    step = lambda w: pl.BlockSpec((1, chunk, w), lambda b, c: (b, c, 0))
    per_b = lambda *s: pl.BlockSpec((1,) + s, lambda b, c: (b,) + (0,) * len(s))
    c0_spec = pl.BlockSpec((1, ML_HEADS, ML_DH, ML_DH), lambda b, c: (c_first + b, 0, 0, 0))
    sds = jax.ShapeDtypeStruct
    return pl.pallas_call(
        functools.partial(_mlstm_kernel, chunk=chunk),
        grid=(bv, sv // chunk),
        in_specs=[step(2 * ML_WIDTH), step(ML_WIDTH), step(LANE),
                  per_b(SUBLANE, 2 * ML_WIDTH), c0_spec, per_b(ML_HEADS, ML_DH), per_b(1, LANE),
                  _const_spec((CONV_W, 2 * ML_WIDTH)), _const_spec((1, 2 * ML_WIDTH)), _const_spec((1, LANE))],
        out_specs=[step(ML_WIDTH), per_b(ML_HEADS, ML_DH, ML_DH), per_b(ML_HEADS, ML_DH), per_b(1, LANE),
                   per_b(SUBLANE, 2 * ML_WIDTH)],
        out_shape=[sds((bv, sv, ML_WIDTH), f32), sds((bv, ML_HEADS, ML_DH, ML_DH), f32), sds((bv, ML_HEADS, ML_DH), f32),
                   sds((bv, 1, LANE), f32), sds((bv, SUBLANE, 2 * ML_WIDTH), f32)],
        scratch_shapes=[pltpu.VMEM((SUBLANE + chunk, 2 * ML_WIDTH), f32), pltpu.VMEM((ML_HEADS, ML_DH, ML_DH), f32),
                        pltpu.VMEM((ML_HEADS, ML_DH), f32), pltpu.VMEM((1, LANE), f32)],
        compiler_params=_params(("parallel", "arbitrary"), 32),
        name="mlstm",
    )(mqk, mv, gate, conv0, c0, n0, m0, W["w_conv"], W["b_conv"], W["b_gate"])


def _outproj_rows(x, a, hm, og, gmh_ref, wa_ref, wm_ref):
    parts = []
    for hd in range(ML_HEADS):
        t = hm[:, hd * ML_DH:(hd + 1) * ML_DH]
        parts.append(t * lax.rsqrt(jnp.mean(t * t, axis=-1, keepdims=True) + EPS))
    hn = jnp.concatenate(parts, axis=1) * gmh_ref[...] * _sigmoid(og)
    return x + _dot(a, wa_ref[...]) + _dot(hn.astype(bf16), wm_ref[...])


def _outproj_kernel(x_ref, a_ref, hm_ref, og_ref, gmh_ref, wa_ref, wm_ref, o_ref):
    o_ref[0] = _outproj_rows(x_ref[0], a_ref[0], hm_ref[0], og_ref[0], gmh_ref, wa_ref, wm_ref)


def _outproj(x, a, hm, og, W):
    bv, sv, _ = x.shape
    tm = min(ROW_TILE, sv)
    hv = MLA_HEADS * V_DIM
    row = lambda w: pl.BlockSpec((1, tm, w), lambda b, j: (b, j, 0))
    return pl.pallas_call(
        _outproj_kernel,
        grid=(bv, sv // tm),
        in_specs=[row(D_MODEL), row(hv), row(ML_WIDTH), row(ML_WIDTH), _const_spec((1, ML_WIDTH)),
                  _const_spec((hv, D_MODEL)), _const_spec((ML_WIDTH, D_MODEL))],
        out_specs=row(D_MODEL),
        out_shape=jax.ShapeDtypeStruct(x.shape, f32),
        compiler_params=_params(("parallel", "parallel"), 32),
        name="outproj",
    )(x, a, hm, og, W["g_mh"], W["w_out_a"], W["w_out_m"])


def _xattn_rows(x, mk_heads, mv_heads, gx_ref, wq_ref, gxq_ref, wo_ref):
    qx = _dot(_rms(x, gx_ref[...]).astype(bf16), wq_ref[...])
    gxq = gxq_ref[...]
    outs = []
    for hd in range(XA_HEADS):
        t = _rms(qx[:, hd * XA_DH:(hd + 1) * XA_DH], gxq) * XA_SCALE
        s = _dot_nt(t.astype(bf16), mk_heads[hd])
        pm = jnp.exp(s - jnp.max(s, axis=-1, keepdims=True))
        outs.append(_dot(pm.astype(bf16), mv_heads[hd]) / jnp.sum(pm, axis=-1, keepdims=True))
    ox = jnp.concatenate(outs, axis=1)
    return x + _dot(ox.astype(bf16), wo_ref[...])


def _xattn_kernel(x_ref, mk_ref, mv_ref, gx_ref, wq_ref, gxq_ref, wo_ref, o_ref):
    mk_heads = [mk_ref[0, :, hd, :].astype(bf16) for hd in range(XA_HEADS)]
    mv_heads = [mv_ref[0, :, hd, :].astype(bf16) for hd in range(XA_HEADS)]
    o_ref[0] = _xattn_rows(x_ref[0], mk_heads, mv_heads, gx_ref, wq_ref, gxq_ref, wo_ref)


def _xattn(x, mem_k, mem_v, W, mem_first=0):
    bv, sv, _ = x.shape
    tm = min(ROW_TILE, sv)
    n_mem = mem_k.shape[1]
    row = pl.BlockSpec((1, tm, D_MODEL), lambda b, j: (b, j, 0))
    mem = pl.BlockSpec((1, n_mem, XA_HEADS, XA_DH), lambda b, j: (mem_first + b, 0, 0, 0))
    return pl.pallas_call(
        _xattn_kernel,
        grid=(bv, sv // tm),
        in_specs=[row, mem, mem, _const_spec((1, D_MODEL)), _const_spec((D_MODEL, D_MODEL)),
                  _const_spec((1, XA_DH)), _const_spec((D_MODEL, D_MODEL))],
        out_specs=row,
        out_shape=jax.ShapeDtypeStruct(x.shape, f32),
        compiler_params=_params(("parallel", "parallel"), 40),
        name="xattn",
    )(x, mem_k, mem_v, W["g_xattn"], W["w_xq"], W["g_xq"], W["w_xo"])


def _mlp_rows(x, g_ref, w1_ref, w2_ref):
    hf = _rms(x, g_ref[...]).astype(bf16)
    acc = x
    for c in range(D_FF // D_MODEL):
        cs = slice(c * D_MODEL, (c + 1) * D_MODEL)
        u = jnp.square(jnp.maximum(_dot(hf, w1_ref[:, cs]), 0.0))
        acc = acc + _dot(u.astype(bf16), w2_ref[cs, :])
    return acc


def _mlp_kernel(x_ref, g_ref, w1_ref, w2_ref, o_ref):
    o_ref[0] = _mlp_rows(x_ref[0], g_ref, w1_ref, w2_ref)


def _mlp(x, W):
    bv, sv, _ = x.shape
    tm = min(ROW_TILE, sv)
    row = pl.BlockSpec((1, tm, D_MODEL), lambda b, j: (b, j, 0))
    return pl.pallas_call(
        _mlp_kernel,
        grid=(bv, sv // tm),
        in_specs=[row, _const_spec((1, D_MODEL)), _const_spec((D_MODEL, D_FF)), _const_spec((D_FF, D_MODEL))],
        out_specs=row,
        out_shape=jax.ShapeDtypeStruct(x.shape, f32),
        compiler_params=_params(("parallel", "parallel"), 48),
        name="mlp",
    )(x, W["g_mlp"], W["w_ff1"], W["w_ff2"])


def _tail_kernel(x_ref, at_ref, hm_ref, og_ref, mk_ref, mv_ref, gmh_ref, wa_ref, wm_ref, gx_ref, wq_ref, gxq_ref,
                 wo_ref, gmlp_ref, w1_ref, w2_ref, o_ref):
    mk, mv = mk_ref[0].astype(bf16), mv_ref[0].astype(bf16)
    mk_heads = [mk[:, hd * XA_DH:(hd + 1) * XA_DH] for hd in range(XA_HEADS)]
    mv_heads = [mv[:, hd * XA_DH:(hd + 1) * XA_DH] for hd in range(XA_HEADS)]
    a = at_ref[0].astype(f32).T.astype(bf16)
    x = _outproj_rows(x_ref[0], a, hm_ref[0], og_ref[0], gmh_ref, wa_ref, wm_ref)
    x = _xattn_rows(x, mk_heads, mv_heads, gx_ref, wq_ref, gxq_ref, wo_ref)
    o_ref[0] = _mlp_rows(x, gmlp_ref, w1_ref, w2_ref)


def _tail(x, at, hm, og, mem_k, mem_v, W):
    bv, sv, _ = x.shape
    tm = min(ROW_TILE, sv)
    hv = MLA_HEADS * V_DIM
    n_mem = mem_k.shape[1]
    row = lambda w: pl.BlockSpec((1, tm, w), lambda b, j: (b, j, 0))
    mem = pl.BlockSpec((1, n_mem, D_MODEL), lambda b, j: (b, 0, 0))
    return pl.pallas_call(
        _tail_kernel,
        grid=(bv, sv // tm),
        in_specs=[row(D_MODEL), pl.BlockSpec((1, hv, tm), lambda b, j: (b, 0, j)), row(ML_WIDTH), row(ML_WIDTH), mem, mem,
                  _const_spec((1, ML_WIDTH)), _const_spec((hv, D_MODEL)), _const_spec((ML_WIDTH, D_MODEL)),
                  _const_spec((1, D_MODEL)), _const_spec((D_MODEL, D_MODEL)), _const_spec((1, XA_DH)),
                  _const_spec((D_MODEL, D_MODEL)),
                  _const_spec((1, D_MODEL)), _const_spec((D_MODEL, D_FF)), _const_spec((D_FF, D_MODEL))],
        out_specs=row(D_MODEL),
        out_shape=jax.ShapeDtypeStruct(x.shape, f32),
        compiler_params=_params(("parallel", "parallel"), 58),
        name="tail",
    )(x, at, hm, og, mem_k, mem_v, W["g_mh"], W["w_out_a"], W["w_out_m"], W["g_xattn"], W["w_xq"], W["g_xq"], W["w_xo"],
      W["g_mlp"], W["w_ff1"], W["w_ff2"])


def _memkv_kernel(mem_ref, gm_ref, wk_ref, wv_ref, gk_ref, k_ref, v_ref):
    hm = _rms(mem_ref[0], gm_ref[...]).astype(bf16)
    k = _dot(hm, wk_ref[...])
    gk = gk_ref[...]
    k_ref[0] = jnp.concatenate([_rms(k[:, hd * XA_DH:(hd + 1) * XA_DH], gk) for hd in range(XA_HEADS)], axis=1)
    v_ref[0] = _dot(hm, wv_ref[...])


def _memkv(mem, W):
    bv, n_mem, _ = mem.shape
    blk = pl.BlockSpec((1, n_mem, D_MODEL), lambda b: (b, 0, 0))
    return pl.pallas_call(
        _memkv_kernel,
        grid=(bv,),
        in_specs=[blk, _const_spec((1, D_MODEL)), _const_spec((D_MODEL, D_MODEL)), _const_spec((D_MODEL, D_MODEL)),
                  _const_spec((1, XA_DH))],
        out_specs=[blk, blk],
        out_shape=[jax.ShapeDtypeStruct(mem.shape, f32)] * 2,
        compiler_params=_params(("parallel",), 32),
        name="memkv",
    )(mem, W["g_mem"], W["w_xk"], W["w_xv"], W["g_xk"])


def _pad_lanes(a, lo, width=LANE):
    pad = [(0, 0)] * (a.ndim - 1) + [(lo, width - lo - a.shape[-1])]
    return jnp.pad(a, pad)


def _rope_tables(pos):
    half = ROPE_DIM // 2
    inv_freq = ROPE_THETA ** (-jnp.arange(half, dtype=f32) / half)
    ang = pos.astype(f32)[:, None] * inv_freq[None, :]
    cos, sin = jnp.cos(ang), jnp.sin(ang)
    n = pos.shape[0]
    ones, zeros = jnp.ones((n, NOPE_DIM), f32), jnp.zeros((n, half), f32)
    tail = jnp.zeros((n, LANE - QK_DIM), f32)
    cos_t = jnp.concatenate([ones, cos, cos, tail], axis=1)
    sin_a = jnp.concatenate([0 * ones, -sin, zeros, tail], axis=1)
    sin_b = jnp.concatenate([0 * ones, zeros, sin, tail], axis=1)
    return cos_t, sin_a, sin_b


def _layer_weights(l, g_mix, w_in, g_qa, w_q_up, g_qnorm, g_kva, w_kv_up, g_knorm, w_conv, b_conv, b_igate, b_fgate,
                   g_mhead, w_out, g_xattn, g_mem, w_xq, w_xk, w_xv, g_xq, g_xk, w_xo, g_mlp, w_ff1, w_ff2):
    wi = w_in[l]
    off_kr = Q_RANK + KV_RANK
    off_mqk = off_kr + ROPE_DIM
    off_mv = off_mqk + 2 * ML_WIDTH
    off_mi = off_mv + ML_WIDTH
    off_mo = off_mi + 2 * ML_HEADS
    packed = jnp.concatenate([
        wi[:, :off_kr],
        _pad_lanes(wi[:, off_kr:off_mqk], NOPE_DIM),
        wi[:, off_mqk:off_mi],
        wi[:, off_mo:],
        _pad_lanes(wi[:, off_mi:off_mo], 0),
    ], axis=1)
    wq = w_q_up[l].reshape(Q_RANK, MLA_HEADS, QK_DIM)
    wkv = w_kv_up[l].reshape(KV_RANK, MLA_HEADS, NOPE_DIM + V_DIM)
    wv = wkv[:, :, NOPE_DIM:].reshape(KV_RANK, MLA_HEADS * V_DIM)
    row = lambda a: a.reshape(1, -1)
    return {
        "g_mix": row(g_mix[l]), "w_in": packed.astype(bf16), "g_qa": row(g_qa[l]),
        "w_q": _pad_lanes(wq, 0).reshape(Q_RANK, MLA_HEADS * LANE).astype(bf16),
        "g_qn": _pad_lanes(row(g_qnorm[l]), 0), "g_kva": row(g_kva[l]),
        "w_k": _pad_lanes(wkv[:, :, :NOPE_DIM], 0).reshape(KV_RANK, MLA_HEADS * LANE).astype(bf16),
        "w_v": wv.astype(bf16), "w_vT": wv.T.astype(bf16),
        "g_kn": _pad_lanes(row(g_knorm[l]), 0),
        "w_conv": w_conv[l], "b_conv": row(b_conv[l]),
        "b_gate": _pad_lanes(row(jnp.concatenate([b_igate[l], b_fgate[l]])), 0),
        "g_mh": row(g_mhead[l]),
        "w_out_a": w_out[l][:MLA_HEADS * V_DIM].astype(bf16), "w_out_m": w_out[l][MLA_HEADS * V_DIM:].astype(bf16),
        "g_xattn": row(g_xattn[l]), "g_mem": row(g_mem[l]),
        "w_xq": w_xq[l].astype(bf16), "w_xk": w_xk[l].astype(bf16), "w_xv": w_xv[l].astype(bf16),
        "g_xq": row(g_xq[l]), "g_xk": row(g_xk[l]), "w_xo": w_xo[l].astype(bf16),
        "g_mlp": row(g_mlp[l]), "w_ff1": w_ff1[l].astype(bf16), "w_ff2": w_ff2[l].astype(bf16),
    }


def _unpack_state(c_new, n_new, m_new, conv_new):
    return (c_new, n_new, m_new[:, 0, GATE_F:GATE_F + ML_HEADS], conv_new[:, SUBLANE - (CONV_W - 1):, :])


def kernel(x_prompt, x_sample, cache_mla_ckv, cache_mla_krope, state_mlstm_C, state_mlstm_n, state_mlstm_m,
           state_mlstm_conv, cache_mem_k, cache_mem_v, mem_prompt, g_mix, w_in, g_qa, w_q_up, g_qnorm, g_kva,
           w_kv_up, g_knorm, w_conv, b_conv, b_igate, b_fgate, g_mhead, w_out, g_xattn, g_mem, w_xq, w_xk, w_xv,
           g_xq, g_xk, w_xo, g_mlp, w_ff1, w_ff2):
    depth = w_in.shape[0]
    bp, sp, _ = x_prompt.shape
    bs, ss, _ = x_sample.shape
    past = cache_mla_ckv.shape[2]
    n_mem = mem_prompt.shape[1]
    weights = (g_mix, w_in, g_qa, w_q_up, g_qnorm, g_kva, w_kv_up, g_knorm, w_conv, b_conv, b_igate, b_fgate,
               g_mhead, w_out, g_xattn, g_mem, w_xq, w_xk, w_xv, g_xq, g_xk, w_xo, g_mlp, w_ff1, w_ff2)

    tabs_p = _rope_tables(jnp.arange(sp))
    tabs_s = tuple(jnp.tile(t, (bs, 1)) for t in _rope_tables(past + jnp.arange(ss)))
    ml_chunk = min(ML_CHUNK, sp)
    cache_ckv = cache_mla_ckv.reshape(depth * bs, past, KV_RANK)
    cache_kr = _pad_lanes(cache_mla_krope, NOPE_DIM).reshape(depth * bs, past, LANE)
    cache_c = state_mlstm_C.reshape(depth * bs, ML_HEADS, ML_DH, ML_DH)
    cache_mk = cache_mem_k.reshape(depth * bs, n_mem, XA_HEADS, XA_DH)
    cache_mv = cache_mem_v.reshape(depth * bs, n_mem, XA_HEADS, XA_DH)

    xp, xs = x_prompt, x_sample.reshape(1, bs * ss, D_MODEL)
    p_out = [[] for _ in range(8)]
    s_out = [[] for _ in range(6)]
    for l in range(depth):
        W = _layer_weights(l, *weights)

        mem_k, mem_v = _memkv(mem_prompt, W)
        q, ckv, kr, mqk, mv, og, gate = _inproj(xp, tabs_p, W)
        k, vt = _kvup(ckv, kr, W, True)
        bound = jnp.max(jnp.abs(g_qnorm[l])) * jnp.max(jnp.abs(g_knorm[l])) * (QK_DIM * MLA_SCALE * LOG2E * 1.05)
        a = lax.cond(bound <= SAFE_LOG2_SCORE, functools.partial(_mla_prompt, bounded=True),
                     functools.partial(_mla_prompt, bounded=False), q, k, vt)
        zeros = lambda *s: jnp.zeros((bp,) + s, f32)
        hm, *state = _mlstm(mqk, mv, gate, zeros(SUBLANE, 2 * ML_WIDTH), zeros(ML_HEADS, ML_DH, ML_DH),
                            zeros(ML_HEADS, ML_DH), zeros(1, LANE), W, ml_chunk)
        xp = _tail(xp, a, hm, og, mem_k, mem_v, W)
        new = (ckv, kr[..., NOPE_DIM:QK_DIM]) + _unpack_state(*state) + (
            mem_k.reshape(bp, n_mem, XA_HEADS, XA_DH), mem_v.reshape(bp, n_mem, XA_HEADS, XA_DH))
        for lst, t in zip(p_out, new):
            lst.append(t)

        q, ckv, kr, mqk, mv, og, gate = _inproj(xs, tabs_s, W)
        per_stream = lambda t: t.reshape(bs, ss, t.shape[-1])
        ckv, kr = per_stream(ckv), per_stream(kr)
        kn, vn = _kvup(ckv, kr, W, False)
        kp, vp = _kvup(cache_ckv, cache_kr, W, False, first=l * bs, count=bs)
        a = _mla_sample(q, kp, vp, kn, vn)
        conv0 = jnp.pad(state_mlstm_conv[l], ((0, 0), (SUBLANE - (CONV_W - 1), 0), (0, 0)))
        m0 = _pad_lanes(state_mlstm_m[l], GATE_F).reshape(bs, 1, LANE)
        hm, *state = _mlstm(per_stream(mqk), per_stream(mv), per_stream(gate), conv0, cache_c,
                            state_mlstm_n[l], m0, W, ss, c_first=l * bs)
        flat = lambda t: t.reshape(1, bs * ss, t.shape[-1])
        xs = _outproj(xs, flat(a), flat(hm), og, W)
        xs = flat(_xattn(per_stream(xs), cache_mk, cache_mv, W, mem_first=l * bs))
        xs = _mlp(xs, W)
        new = (ckv, kr[..., NOPE_DIM:QK_DIM]) + _unpack_state(*state)
        for lst, t in zip(s_out, new):
            lst.append(t)

    outs_p = tuple(jnp.stack(t) for t in p_out)
    outs_s = tuple(jnp.stack(t) for t in s_out)
    return (xp, xs.reshape(bs, ss, D_MODEL)) + outs_p + outs_s
```

```python
import functools

import jax
import jax.numpy as jnp
import numpy as np
from jax import lax
from jax.experimental import pallas as pl
from jax.experimental.pallas import tpu as pltpu

f32 = jnp.float32
bf16 = jnp.bfloat16

D_MODEL = 1024
CHUNK = 64
EPS = 1e-6
MLA_HEADS = 8
Q_RANK = 256
KV_RANK = 128
NOPE_DIM = 64
ROPE_DIM = 32
QK_DIM = NOPE_DIM + ROPE_DIM
V_DIM = 64
ROPE_THETA = 10000.0
MLA_SCALE = QK_DIM ** -0.5
ML_HEADS = 4
ML_DH = 128
ML_WIDTH = ML_HEADS * ML_DH
CONV_W = 4
XA_HEADS = 4
XA_DH = D_MODEL // XA_HEADS
XA_SCALE = XA_DH ** -0.5
D_FF = 4 * D_MODEL

LANE = 128
SUBLANE = 8
MIB = 1024 * 1024

C_QA = 0
C_KVA = C_QA + Q_RANK
C_KR = C_KVA + KV_RANK
C_MQK = C_KR + LANE
C_MV = C_MQK + 2 * ML_WIDTH
C_MO = C_MV + ML_WIDTH
C_GATE = C_MO + ML_WIDTH
C_KR_PARTNER = C_GATE + LANE
IN_COLS_PACKED = C_KR_PARTNER + LANE
GATE_F = ML_HEADS

ROW_TILE = 512
ATT_TILE = 2048
ATT_BLOCK = 1024
ATT_SUB = 512
FAST_HEADS_PER_TRIP = 4
ML_CHUNK = 256

BF16_ROWS = 16
V_EXT = V_DIM + BF16_ROWS
LOG2E = 1.4426950408889634
SAFE_LOG2_SCORE = 64.0

NT_DIMS = (((1,), (1,)), ((), ()))


def _rms(x, g):
    return x * lax.rsqrt(jnp.mean(x * x, axis=-1, keepdims=True) + EPS) * g


def _dot(a, b):
    return jnp.dot(a, b, preferred_element_type=f32)


def _dot_nt(a, b):
    return lax.dot_general(a, b, NT_DIMS, preferred_element_type=f32)


def _sigmoid(x):
    return 1.0 / (1.0 + jnp.exp(-x))


def _const_spec(shape):
    nd = len(shape)
    return pl.BlockSpec(shape, lambda *_: (0,) * nd, pipeline_mode=pl.Buffered(1))


def _params(semantics, vmem_mib):
    return pltpu.CompilerParams(dimension_semantics=semantics, vmem_limit_bytes=vmem_mib * MIB)


def _inproj_kernel(x_ref, cos_ref, sin_ref, gmix_ref, win_ref, gqa_ref, wq_ref, gqn_ref, gkva_ref,
                   q_ref, ckv_ref, kr_ref, mqk_ref, mv_ref, og_ref, gate_ref):
    h = _rms(x_ref[0], gmix_ref[...]).astype(bf16)
    proj = lambda lo, hi: _dot(h, win_ref[:, lo:hi])
    cos, sin = cos_ref[...], sin_ref[...]
    q_lat = _rms(proj(C_QA, C_KVA), gqa_ref[...])
    qf = _dot(q_lat.astype(bf16), wq_ref[...])
    gq = gqn_ref[...] * (MLA_SCALE * LOG2E)
    for hd in range(MLA_HEADS):
        lanes = slice(hd * LANE, (hd + 1) * LANE)
        partner = slice((MLA_HEADS + hd) * LANE, (MLA_HEADS + hd + 1) * LANE)
        t = qf[:, lanes] * cos + qf[:, partner] * sin
        ss = jnp.sum(t * t, axis=-1, keepdims=True) * (1.0 / QK_DIM)
        q_ref[0, hd] = (t * lax.rsqrt(ss + EPS) * gq).astype(bf16)
    latent = proj(C_KVA, C_MQK)
    gate_and_partner = proj(C_GATE, IN_COLS_PACKED)
    ckv_ref[0] = _rms(latent[:, :KV_RANK], gkva_ref[...])
    kr_ref[0] = latent[:, KV_RANK:] * cos + gate_and_partner[:, LANE:] * sin
    mqk_ref[0] = proj(C_MQK, C_MV)
    mv_ref[0] = proj(C_MV, C_MO)
    og_ref[0] = proj(C_MO, C_GATE)
    gate_ref[0] = gate_and_partner[:, :LANE]


def _inproj(x, tabs, W):
    bv, sv, _ = x.shape
    tm = min(ROW_TILE, sv)
    row = lambda w: pl.BlockSpec((1, tm, w), lambda b, j: (b, j, 0))
    tab = pl.BlockSpec((tm, LANE), lambda b, j: (j, 0))
    sds = jax.ShapeDtypeStruct
    return pl.pallas_call(
        _inproj_kernel,
        grid=(bv, sv // tm),
        in_specs=[row(D_MODEL), tab, tab,
                  _const_spec((1, D_MODEL)), _const_spec((D_MODEL, IN_COLS_PACKED)),
                  _const_spec((1, Q_RANK)), _const_spec((Q_RANK, 2 * MLA_HEADS * LANE)),
                  _const_spec((1, LANE)), _const_spec((1, KV_RANK))],
        out_specs=[pl.BlockSpec((1, MLA_HEADS, tm, LANE), lambda b, j: (b, 0, j, 0)),
                   row(KV_RANK), row(LANE), row(2 * ML_WIDTH), row(ML_WIDTH), row(ML_WIDTH), row(LANE)],
        out_shape=[sds((bv, MLA_HEADS, sv, LANE), bf16), sds((bv, sv, KV_RANK), f32), sds((bv, sv, LANE), f32),
                   sds((bv, sv, 2 * ML_WIDTH), f32), sds((bv, sv, ML_WIDTH), f32), sds((bv, sv, ML_WIDTH), f32),
                   sds((bv, sv, LANE), f32)],
        compiler_params=_params(("parallel", "parallel"), 48),
        name="inproj",
    )(x, *tabs, W["g_mix"], W["w_in"], W["g_qa"], W["w_q"], W["g_qn"], W["g_kva"])


def _kvup_kernel(ckv_ref, kr_ref, wk_ref, wv_ref, gk_ref, k_ref, v_ref, *, transposed_v):
    c = ckv_ref[0].astype(bf16)
    kn = _dot(c, wk_ref[...])
    kr = kr_ref[0]
    g = gk_ref[...]
    for hd in range(MLA_HEADS):
        t = kn[:, hd * LANE:(hd + 1) * LANE] + kr
        ss = jnp.sum(t * t, axis=-1, keepdims=True) * (1.0 / QK_DIM)
        k_ref[0, hd] = (t * lax.rsqrt(ss + EPS) * g).astype(bf16)
    if transposed_v:
        vt = _dot_nt(wv_ref[...], c).astype(bf16)
        tm = vt.shape[1]
        ones_tile = (lax.broadcasted_iota(jnp.int32, (BF16_ROWS, tm), 0) == 0).astype(bf16)
        for hd in range(MLA_HEADS):
            v_ref[0, hd, 0:V_DIM, :] = vt[hd * V_DIM:(hd + 1) * V_DIM, :]
            v_ref[0, hd, V_DIM:V_EXT, :] = ones_tile
    else:
        v_ref[0] = _dot(c, wv_ref[...]).astype(bf16)


def _kvup(ckv, kr, W, transposed_v, first=0, count=None):
    _, sv, _ = ckv.shape
    bv = ckv.shape[0] if count is None else count
    tm = min(ROW_TILE, sv)
    hv = MLA_HEADS * V_DIM
    row_in = lambda w: pl.BlockSpec((1, tm, w), lambda b, j: (first + b, j, 0))
    row = lambda w: pl.BlockSpec((1, tm, w), lambda b, j: (b, j, 0))
    out_specs = [pl.BlockSpec((1, MLA_HEADS, tm, LANE), lambda b, j: (b, 0, j, 0))]
    out_shape = [jax.ShapeDtypeStruct((bv, MLA_HEADS, sv, LANE), bf16)]
    if transposed_v:
        out_specs += [pl.BlockSpec((1, MLA_HEADS, V_EXT, tm), lambda b, j: (b, 0, 0, j))]
        out_shape += [jax.ShapeDtypeStruct((bv, MLA_HEADS, V_EXT, sv), bf16)]
        wv = W["w_vT"]
    else:
        out_specs += [row(hv)]
        out_shape += [jax.ShapeDtypeStruct((bv, sv, hv), bf16)]
        wv = W["w_v"]
    return pl.pallas_call(
        functools.partial(_kvup_kernel, transposed_v=transposed_v),
        grid=(bv, sv // tm),
        in_specs=[row_in(KV_RANK), row_in(LANE), _const_spec((KV_RANK, MLA_HEADS * LANE)), _const_spec(wv.shape),
                  _const_spec((1, LANE))],
        out_specs=out_specs,
        out_shape=out_shape,
        compiler_params=_params(("parallel", "parallel"), 32),
        name="kvup_t" if transposed_v else "kvup",
    )(ckv, kr, W["w_k"], wv, W["g_kn"])


def _mla_prompt_kernel(qi_ref, ki_ref, q_ref, k_ref, vt_ref, o_ref, m_sc, acc_sc, *, tile, sub, bounded):
    p = pl.program_id(1)
    qi, ki = qi_ref[p], ki_ref[p]
    nsub = tile // sub
    blk = min(ATT_BLOCK, tile)
    blk_key_chunk = lax.broadcasted_iota(jnp.int32, (blk, 1), 0) // CHUNK
    blk_query_chunk = lax.broadcasted_iota(jnp.int32, (1, blk), 1) // CHUNK
    query_chunk = lax.broadcasted_iota(jnp.int32, (1, sub), 1) // CHUNK

    @pl.when(ki == 0)
    def _():
        m_sc[...] = jnp.full(m_sc.shape, -jnp.inf, f32)
        acc_sc[...] = jnp.zeros(acc_sc.shape, f32)

    def scores(hd, lo, hi, cols):
        return _dot_nt(k_ref[0, hd, lo:hi, :], q_ref[0, hd, cols, :])

    def pv(hd, lo, hi, pm):
        return _dot(vt_ref[0, hd, :, lo:hi], pm.astype(bf16))

    def finish(hd, cols, acc):
        row0 = hd * V_DIM if isinstance(hd, int) else pl.multiple_of(hd * V_DIM, V_DIM)
        o_ref[0, pl.ds(row0, V_DIM), cols] = (acc[:V_DIM] / acc[V_DIM:V_DIM + 1]).astype(o_ref.dtype)

    def fast_step(hd, diag):
        for qs in range(tile // blk):
            cols = slice(qs * blk, (qs + 1) * blk)
            acc = acc_sc[hd, :, cols]
            for kb in range(qs if diag else tile // blk):
                lo = kb * blk
                acc = acc + pv(hd, lo, lo + blk, jnp.exp2(scores(hd, lo, lo + blk, cols)))
            if diag:
                lo = qs * blk
                pm = jnp.where(blk_key_chunk <= blk_query_chunk, jnp.exp2(scores(hd, lo, lo + blk, cols)), 0.0)
                finish(hd, cols, acc + pv(hd, lo, lo + blk, pm))
            else:
                acc_sc[hd, :, cols] = acc

    def slow_step(hd, diag):
        for qs in range(nsub):
            cols = slice(qs * sub, (qs + 1) * sub)
            kv_len = (qs + 1) * sub if diag else tile
            s = scores(hd, 0, kv_len, cols)
            if diag:
                kc = lax.broadcasted_iota(jnp.int32, (kv_len, 1), 0) // CHUNK
                s = jnp.where(kc <= query_chunk + (qs * sub) // CHUNK, s, -jnp.inf)
            m_old = m_sc[hd, :, cols]
            m_new = jnp.maximum(m_old, jnp.max(s, axis=0, keepdims=True))
            acc = jnp.exp2(m_old - m_new) * acc_sc[hd, :, cols] + pv(hd, 0, kv_len, jnp.exp2(s - m_new))
            if diag:
                finish(hd, cols, acc)
            else:
                m_sc[hd, :, cols] = m_new
                acc_sc[hd, :, cols] = acc

    for diag, where in ((False, ki != qi), (True, ki == qi)):
        @pl.when(where)
        def _(diag=diag):
            if bounded:
                lax.fori_loop(0, MLA_HEADS, lambda hd, c: (fast_step(hd, diag), c)[1], 0, unroll=FAST_HEADS_PER_TRIP)
            else:
                lax.fori_loop(0, MLA_HEADS, lambda hd, c: (slow_step(hd, diag), c)[1], 0)


def _mla_prompt(q, k, vt, bounded):
    bv, _, sv, _ = q.shape
    tile = min(ATT_TILE, sv)
    sub = min(ATT_SUB, tile)
    nq = sv // tile
    pairs = [(i, j) for i in range(nq) for j in range(i + 1)]
    qi = jnp.asarray(np.array([a for a, _ in pairs], np.int32))
    ki = jnp.asarray(np.array([b for _, b in pairs], np.int32))
    hv = MLA_HEADS * V_DIM
    grid_spec = pltpu.PrefetchScalarGridSpec(
        num_scalar_prefetch=2,
        grid=(bv, len(pairs)),
        in_specs=[pl.BlockSpec((1, MLA_HEADS, tile, LANE), lambda b, p, qi, ki: (b, 0, qi[p], 0)),
                  pl.BlockSpec((1, MLA_HEADS, tile, LANE), lambda b, p, qi, ki: (b, 0, ki[p], 0)),
                  pl.BlockSpec((1, MLA_HEADS, V_EXT, tile), lambda b, p, qi, ki: (b, 0, 0, ki[p]))],
        out_specs=pl.BlockSpec((1, hv, tile), lambda b, p, qi, ki: (b, 0, qi[p])),
        scratch_shapes=[pltpu.VMEM((MLA_HEADS, 1, tile), f32), pltpu.VMEM((MLA_HEADS, V_EXT, tile), f32)],
    )
    return pl.pallas_call(
        functools.partial(_mla_prompt_kernel, tile=tile, sub=sub, bounded=bounded),
        grid_spec=grid_spec,
        out_shape=jax.ShapeDtypeStruct((bv, hv, sv), bf16),
        compiler_params=_params(("parallel", "arbitrary"), 56),
        name="mla_prompt" if bounded else "mla_prompt_running_max",
    )(qi, ki, q, k, vt)


def _mla_sample_kernel(q_ref, kp_ref, vp_ref, kn_ref, vn_ref, o_ref):
    vp = vp_ref[0]
    vn = vn_ref[0]
    lane_head = lax.broadcasted_iota(jnp.int32, (1, MLA_HEADS * V_DIM), 1) // V_DIM
    out = jnp.zeros(o_ref.shape[1:], f32)
    for hd in range(MLA_HEADS):
        qh = q_ref[0, hd]
        s1 = _dot_nt(qh, kp_ref[0, hd])
        s2 = _dot_nt(qh, kn_ref[0, hd])
        m = jnp.maximum(jnp.max(s1, axis=-1, keepdims=True), jnp.max(s2, axis=-1, keepdims=True))
        p1 = jnp.exp2(s1 - m)
        p2 = jnp.exp2(s2 - m)
        l = jnp.sum(p1, axis=-1, keepdims=True) + jnp.sum(p2, axis=-1, keepdims=True)
        sel = lane_head == hd
        o = _dot(p1.astype(bf16), jnp.where(sel, vp, 0)) + _dot(p2.astype(bf16), jnp.where(sel, vn, 0))
        out = out + o / l
    o_ref[0] = out.astype(o_ref.dtype)


def _mla_sample(q, kp, vp, kn, vn):
    bv, _, sv, _ = kn.shape
    past = kp.shape[2]
    hv = MLA_HEADS * V_DIM
    return pl.pallas_call(
        _mla_sample_kernel,
        grid=(bv,),
        in_specs=[pl.BlockSpec((1, MLA_HEADS, sv, LANE), lambda b: (0, 0, b, 0)),
                  pl.BlockSpec((1, MLA_HEADS, past, LANE), lambda b: (b, 0, 0, 0)),
                  pl.BlockSpec((1, past, hv), lambda b: (b, 0, 0)),
                  pl.BlockSpec((1, MLA_HEADS, sv, LANE), lambda b: (b, 0, 0, 0)),
                  pl.BlockSpec((1, sv, hv), lambda b: (b, 0, 0))],
        out_specs=pl.BlockSpec((1, sv, hv), lambda b: (b, 0, 0)),
        out_shape=jax.ShapeDtypeStruct((bv, sv, hv), bf16),
        compiler_params=_params(("parallel",), 32),
        name="mla_sample",
    )(q, kp, vp, kn, vn)


def _mlstm_kernel(mqk_ref, mv_ref, gate_ref, conv0_ref, c0_ref, n0_ref, m0_ref, wconv_ref, bconv_ref, bgate_ref,
                  h_ref, cout_ref, nout_ref, mout_ref, convout_ref, ext_sc, c_sc, n_sc, m_sc, *, chunk):
    step = pl.program_id(1)
    L = chunk

    @pl.when(step == 0)
    def _():
        ext_sc[0:SUBLANE, :] = conv0_ref[0]
        c_sc[...] = c0_ref[0]
        n_sc[...] = n0_ref[0]
        m_sc[...] = m0_ref[0]

    x = mqk_ref[0]
    ext_sc[SUBLANE:SUBLANE + L, :] = x
    y = bconv_ref[...] + x * wconv_ref[CONV_W - 1:CONV_W, :]
    for j in range(CONV_W - 1):
        off = SUBLANE - (CONV_W - 1) + j
        y = y + ext_sc[off:off + L, :] * wconv_ref[j:j + 1, :]
    ext_sc[0:SUBLANE, :] = x[L - SUBLANE:L, :]
    qk = y * _sigmoid(y)
    mq = qk[:, :ML_WIDTH]
    mk = qk[:, ML_WIDTH:] * (ML_DH ** -0.5)
    mv = mv_ref[0]

    g = gate_ref[0] + bgate_ref[...]
    ls = jnp.minimum(g, 0.0) - jnp.log1p(jnp.exp(-jnp.abs(g)))
    r_i = lax.broadcasted_iota(jnp.int32, (L, L), 0)
    c_i = lax.broadcasted_iota(jnp.int32, (L, L), 1)
    causal = c_i <= r_i
    tri = causal.astype(f32)
    bcum = jnp.dot(tri, ls, preferred_element_type=f32, precision=lax.Precision.HIGHEST)
    g_t = g.T[0:SUBLANE, :]
    bcum_t = lax.dot_general(ls.T[0:SUBLANE, :], tri, NT_DIMS, preferred_element_type=f32,
                             precision=lax.Precision.HIGHEST)

    lane = lax.broadcasted_iota(jnp.int32, (1, LANE), 1)
    sub8 = lax.broadcasted_iota(jnp.int32, (SUBLANE, 1), 0)
    last_row = lax.broadcasted_iota(jnp.int32, (L, 1), 0) == L - 1

    def col(t, idx):
        return jnp.sum(jnp.where(lane == idx, t, 0.0), axis=1, keepdims=True)

    def row(t, idx):
        return jnp.sum(jnp.where(sub8 == idx, t, 0.0), axis=0, keepdims=True)

    m_vec = m_sc[...]
    m_next = jnp.zeros_like(m_vec)
    for hd in range(ML_HEADS):
        hs = slice(hd * ML_DH, (hd + 1) * ML_DH)
        b_col, ig_col = col(bcum, GATE_F + hd), col(g, hd)
        b_row, ig_row = row(bcum_t, GATE_F + hd), row(g_t, hd)
        m_prev = col(m_vec, GATE_F + hd)
        inter = b_col + m_prev
        d = jnp.where(causal, b_col - b_row + ig_row, -jnp.inf)
        m_t = jnp.maximum(inter, jnp.max(d, axis=1, keepdims=True))
        w = jnp.exp(d - m_t)
        a_inter = jnp.exp(inter - m_t)
        qh, kh, vh = mq[:, hs], mk[:, hs], mv[:, hs]
        qb, vb = qh.astype(bf16), vh.astype(bf16)
        sqk = _dot_nt(qb, kh.astype(bf16)) * w
        c_h = c_sc[hd]
        n_h = n_sc[hd:hd + 1, :]
        num = a_inter * _dot(qb, c_h.astype(bf16)) + _dot(sqk.astype(bf16), vb)
        qn = a_inter * jnp.sum(qh * n_h, axis=1, keepdims=True) + jnp.sum(sqk, axis=1, keepdims=True)
        h_ref[0, :, hs] = num / jnp.maximum(jnp.abs(qn), jnp.exp(-m_t))
        b_last = jnp.sum(jnp.where(last_row, b_col, 0.0), axis=0, keepdims=True)
        m_end = jnp.sum(jnp.where(last_row, m_t, 0.0), axis=0, keepdims=True)
        decay = jnp.exp(b_last + m_prev - m_end)
        kw = kh * jnp.exp(b_last - b_col + ig_col - m_end)
        c_sc[hd] = decay * c_h + _dot(kw.T.astype(bf16), vb)
        n_sc[hd:hd + 1, :] = decay * n_h + jnp.sum(kw, axis=0, keepdims=True)
        m_next = m_next + jnp.where(lane == GATE_F + hd, m_end, 0.0)
    m_sc[...] = m_next

    @pl.when(step == pl.num_programs(1) - 1)
    def _():
        cout_ref[0] = c_sc[...]
        nout_ref[0] = n_sc[...]
        mout_ref[0] = m_sc[...]
        convout_ref[0] = ext_sc[0:SUBLANE, :]


def _mlstm(mqk, mv, gate, conv0, c0, n0, m0, W, chunk, c_first=0):
    bv, sv, _ = mqk.shape
    step = lambda w: pl.BlockSpec((1, chunk, w), lambda b, c: (b, c, 0))
    per_b = lambda *s: pl.BlockSpec((1,) + s, lambda b, c: (b,) + (0,) * len(s))
    c0_spec = pl.BlockSpec((1, ML_HEADS, ML_DH, ML_DH), lambda b, c: (c_first + b, 0, 0, 0))
    sds = jax.ShapeDtypeStruct
    return pl.pallas_call(
        functools.partial(_mlstm_kernel, chunk=chunk),
        grid=(bv, sv // chunk),
        in_specs=[step(2 * ML_WIDTH), step(ML_WIDTH), step(LANE),
                  per_b(SUBLANE, 2 * ML_WIDTH), c0_spec, per_b(ML_HEADS, ML_DH), per_b(1, LANE),
                  _const_spec((CONV_W, 2 * ML_WIDTH)), _const_spec((1, 2 * ML_WIDTH)), _const_spec((1, LANE))],
        out_specs=[step(ML_WIDTH), per_b(ML_HEADS, ML_DH, ML_DH), per_b(ML_HEADS, ML_DH), per_b(1, LANE),
                   per_b(SUBLANE, 2 * ML_WIDTH)],
        out_shape=[sds((bv, sv, ML_WIDTH), f32), sds((bv, ML_HEADS, ML_DH, ML_DH), f32), sds((bv, ML_HEADS, ML_DH), f32),
                   sds((bv, 1, LANE), f32), sds((bv, SUBLANE, 2 * ML_WIDTH), f32)],
        scratch_shapes=[pltpu.VMEM((SUBLANE + chunk, 2 * ML_WIDTH), f32), pltpu.VMEM((ML_HEADS, ML_DH, ML_DH), f32),
                        pltpu.VMEM((ML_HEADS, ML_DH), f32), pltpu.VMEM((1, LANE), f32)],
        compiler_params=_params(("parallel", "arbitrary"), 32),
        name="mlstm",
    )(mqk, mv, gate, conv0, c0, n0, m0, W["w_conv"], W["b_conv"], W["b_gate"])


def _outproj_rows(x, a, hm, og, gmh_ref, wa_ref, wm_ref):
    parts = []
    for hd in range(ML_HEADS):
        t = hm[:, hd * ML_DH:(hd + 1) * ML_DH]
        parts.append(t * lax.rsqrt(jnp.mean(t * t, axis=-1, keepdims=True) + EPS))
    hn = jnp.concatenate(parts, axis=1) * gmh_ref[...] * _sigmoid(og)
    return x + _dot(a, wa_ref[...]) + _dot(hn.astype(bf16), wm_ref[...])


def _outproj_kernel(x_ref, a_ref, hm_ref, og_ref, gmh_ref, wa_ref, wm_ref, o_ref):
    o_ref[0] = _outproj_rows(x_ref[0], a_ref[0], hm_ref[0], og_ref[0], gmh_ref, wa_ref, wm_ref)


def _outproj(x, a, hm, og, W):
    bv, sv, _ = x.shape
    tm = min(ROW_TILE, sv)
    hv = MLA_HEADS * V_DIM
    row = lambda w: pl.BlockSpec((1, tm, w), lambda b, j: (b, j, 0))
    return pl.pallas_call(
        _outproj_kernel,
        grid=(bv, sv // tm),
        in_specs=[row(D_MODEL), row(hv), row(ML_WIDTH), row(ML_WIDTH), _const_spec((1, ML_WIDTH)),
                  _const_spec((hv, D_MODEL)), _const_spec((ML_WIDTH, D_MODEL))],
        out_specs=row(D_MODEL),
        out_shape=jax.ShapeDtypeStruct(x.shape, f32),
        compiler_params=_params(("parallel", "parallel"), 32),
        name="outproj",
    )(x, a, hm, og, W["g_mh"], W["w_out_a"], W["w_out_m"])


def _xattn_rows(x, mk_heads, mv_heads, gx_ref, wq_ref, gxq_ref, wo_ref):
    qx = _dot(_rms(x, gx_ref[...]).astype(bf16), wq_ref[...])
    gxq = gxq_ref[...]
    outs = []
    for hd in range(XA_HEADS):
        t = _rms(qx[:, hd * XA_DH:(hd + 1) * XA_DH], gxq) * XA_SCALE
        s = _dot_nt(t.astype(bf16), mk_heads[hd])
        pm = jnp.exp(s - jnp.max(s, axis=-1, keepdims=True))
        outs.append(_dot(pm.astype(bf16), mv_heads[hd]) / jnp.sum(pm, axis=-1, keepdims=True))
    ox = jnp.concatenate(outs, axis=1)
    return x + _dot(ox.astype(bf16), wo_ref[...])


def _xattn_kernel(x_ref, mk_ref, mv_ref, gx_ref, wq_ref, gxq_ref, wo_ref, o_ref):
    mk_heads = [mk_ref[0, :, hd, :].astype(bf16) for hd in range(XA_HEADS)]
    mv_heads = [mv_ref[0, :, hd, :].astype(bf16) for hd in range(XA_HEADS)]
    o_ref[0] = _xattn_rows(x_ref[0], mk_heads, mv_heads, gx_ref, wq_ref, gxq_ref, wo_ref)


def _xattn(x, mem_k, mem_v, W, mem_first=0):
    bv, sv, _ = x.shape
    tm = min(ROW_TILE, sv)
    n_mem = mem_k.shape[1]
    row = pl.BlockSpec((1, tm, D_MODEL), lambda b, j: (b, j, 0))
    mem = pl.BlockSpec((1, n_mem, XA_HEADS, XA_DH), lambda b, j: (mem_first + b, 0, 0, 0))
    return pl.pallas_call(
        _xattn_kernel,
        grid=(bv, sv // tm),
        in_specs=[row, mem, mem, _const_spec((1, D_MODEL)), _const_spec((D_MODEL, D_MODEL)),
                  _const_spec((1, XA_DH)), _const_spec((D_MODEL, D_MODEL))],
        out_specs=row,
        out_shape=jax.ShapeDtypeStruct(x.shape, f32),
        compiler_params=_params(("parallel", "parallel"), 40),
        name="xattn",
    )(x, mem_k, mem_v, W["g_xattn"], W["w_xq"], W["g_xq"], W["w_xo"])


def _mlp_rows(x, g_ref, w1_ref, w2_ref):
    hf = _rms(x, g_ref[...]).astype(bf16)
    acc = x
    for c in range(D_FF // D_MODEL):
        cs = slice(c * D_MODEL, (c + 1) * D_MODEL)
        u = jnp.square(jnp.maximum(_dot(hf, w1_ref[:, cs]), 0.0))
        acc = acc + _dot(u.astype(bf16), w2_ref[cs, :])
    return acc


def _mlp_kernel(x_ref, g_ref, w1_ref, w2_ref, o_ref):
    o_ref[0] = _mlp_rows(x_ref[0], g_ref, w1_ref, w2_ref)


def _mlp(x, W):
    bv, sv, _ = x.shape
    tm = min(ROW_TILE, sv)
    row = pl.BlockSpec((1, tm, D_MODEL), lambda b, j: (b, j, 0))
    return pl.pallas_call(
        _mlp_kernel,
        grid=(bv, sv // tm),
        in_specs=[row, _const_spec((1, D_MODEL)), _const_spec((D_MODEL, D_FF)), _const_spec((D_FF, D_MODEL))],
        out_specs=row,
        out_shape=jax.ShapeDtypeStruct(x.shape, f32),
        compiler_params=_params(("parallel", "parallel"), 48),
        name="mlp",
    )(x, W["g_mlp"], W["w_ff1"], W["w_ff2"])


def _tail_kernel(x_ref, at_ref, hm_ref, og_ref, mk_ref, mv_ref, gmh_ref, wa_ref, wm_ref, gx_ref, wq_ref, gxq_ref,
                 wo_ref, gmlp_ref, w1_ref, w2_ref, o_ref):
    mk, mv = mk_ref[0].astype(bf16), mv_ref[0].astype(bf16)
    mk_heads = [mk[:, hd * XA_DH:(hd + 1) * XA_DH] for hd in range(XA_HEADS)]
    mv_heads = [mv[:, hd * XA_DH:(hd + 1) * XA_DH] for hd in range(XA_HEADS)]
    a = at_ref[0].astype(f32).T.astype(bf16)
    x = _outproj_rows(x_ref[0], a, hm_ref[0], og_ref[0], gmh_ref, wa_ref, wm_ref)
    x = _xattn_rows(x, mk_heads, mv_heads, gx_ref, wq_ref, gxq_ref, wo_ref)
    o_ref[0] = _mlp_rows(x, gmlp_ref, w1_ref, w2_ref)


def _tail(x, at, hm, og, mem_k, mem_v, W):
    bv, sv, _ = x.shape
    tm = min(ROW_TILE, sv)
    hv = MLA_HEADS * V_DIM
    n_mem = mem_k.shape[1]
    row = lambda w: pl.BlockSpec((1, tm, w), lambda b, j: (b, j, 0))
    mem = pl.BlockSpec((1, n_mem, D_MODEL), lambda b, j: (b, 0, 0))
    return pl.pallas_call(
        _tail_kernel,
        grid=(bv, sv // tm),
        in_specs=[row(D_MODEL), pl.BlockSpec((1, hv, tm), lambda b, j: (b, 0, j)), row(ML_WIDTH), row(ML_WIDTH), mem, mem,
                  _const_spec((1, ML_WIDTH)), _const_spec((hv, D_MODEL)), _const_spec((ML_WIDTH, D_MODEL)),
                  _const_spec((1, D_MODEL)), _const_spec((D_MODEL, D_MODEL)), _const_spec((1, XA_DH)),
                  _const_spec((D_MODEL, D_MODEL)),
                  _const_spec((1, D_MODEL)), _const_spec((D_MODEL, D_FF)), _const_spec((D_FF, D_MODEL))],
        out_specs=row(D_MODEL),
        out_shape=jax.ShapeDtypeStruct(x.shape, f32),
        compiler_params=_params(("parallel", "parallel"), 58),
        name="tail",
    )(x, at, hm, og, mem_k, mem_v, W["g_mh"], W["w_out_a"], W["w_out_m"], W["g_xattn"], W["w_xq"], W["g_xq"], W["w_xo"],
      W["g_mlp"], W["w_ff1"], W["w_ff2"])


def _memkv_kernel(mem_ref, gm_ref, wk_ref, wv_ref, gk_ref, k_ref, v_ref):
    hm = _rms(mem_ref[0], gm_ref[...]).astype(bf16)
    k = _dot(hm, wk_ref[...])
    gk = gk_ref[...]
    k_ref[0] = jnp.concatenate([_rms(k[:, hd * XA_DH:(hd + 1) * XA_DH], gk) for hd in range(XA_HEADS)], axis=1)
    v_ref[0] = _dot(hm, wv_ref[...])


def _memkv(mem, W):
    bv, n_mem, _ = mem.shape
    blk = pl.BlockSpec((1, n_mem, D_MODEL), lambda b: (b, 0, 0))
    return pl.pallas_call(
        _memkv_kernel,
        grid=(bv,),
        in_specs=[blk, _const_spec((1, D_MODEL)), _const_spec((D_MODEL, D_MODEL)), _const_spec((D_MODEL, D_MODEL)),
                  _const_spec((1, XA_DH))],
        out_specs=[blk, blk],
        out_shape=[jax.ShapeDtypeStruct(mem.shape, f32)] * 2,
        compiler_params=_params(("parallel",), 32),
        name="memkv",
    )(mem, W["g_mem"], W["w_xk"], W["w_xv"], W["g_xk"])


def _pad_lanes(a, lo, width=LANE):
    pad = [(0, 0)] * (a.ndim - 1) + [(lo, width - lo - a.shape[-1])]
    return jnp.pad(a, pad)


def _rope_tables(pos):
    half = ROPE_DIM // 2
    inv_freq = ROPE_THETA ** (-jnp.arange(half, dtype=f32) / half)
    ang = pos.astype(f32)[:, None] * inv_freq[None, :]
    cos, sin = jnp.cos(ang), jnp.sin(ang)
    n = pos.shape[0]
    ones = jnp.ones((n, NOPE_DIM), f32)
    tail = jnp.zeros((n, LANE - QK_DIM), f32)
    cos_t = jnp.concatenate([ones, cos, cos, tail], axis=1)
    sin_t = jnp.concatenate([0 * ones, sin, sin, tail], axis=1)
    return cos_t, sin_t


def _rope_partner(w):
    half = ROPE_DIM // 2
    x1, x2 = w[..., -ROPE_DIM:-half], w[..., -half:]
    return jnp.concatenate([jnp.zeros_like(w[..., :-ROPE_DIM]), -x2, x1], axis=-1)


def _layer_weights(l, g_mix, w_in, g_qa, w_q_up, g_qnorm, g_kva, w_kv_up, g_knorm, w_conv, b_conv, b_igate, b_fgate,
                   g_mhead, w_out, g_xattn, g_mem, w_xq, w_xk, w_xv, g_xq, g_xk, w_xo, g_mlp, w_ff1, w_ff2):
    wi = w_in[l]
    off_kr = Q_RANK + KV_RANK
    off_mqk = off_kr + ROPE_DIM
    off_mv = off_mqk + 2 * ML_WIDTH
    off_mi = off_mv + ML_WIDTH
    off_mo = off_mi + 2 * ML_HEADS
    w_kr = wi[:, off_kr:off_mqk]
    packed = jnp.concatenate([
        wi[:, :off_kr],
        _pad_lanes(w_kr, NOPE_DIM),
        wi[:, off_mqk:off_mi],
        wi[:, off_mo:],
        _pad_lanes(wi[:, off_mi:off_mo], 0),
        _pad_lanes(_rope_partner(w_kr), NOPE_DIM),
    ], axis=1)
    wq = w_q_up[l].reshape(Q_RANK, MLA_HEADS, QK_DIM)
    wq = jnp.concatenate([_pad_lanes(wq, 0).reshape(Q_RANK, MLA_HEADS * LANE),
                          _pad_lanes(_rope_partner(wq), 0).reshape(Q_RANK, MLA_HEADS * LANE)], axis=1)
    wkv = w_kv_up[l].reshape(KV_RANK, MLA_HEADS, NOPE_DIM + V_DIM)
    wv = wkv[:, :, NOPE_DIM:].reshape(KV_RANK, MLA_HEADS * V_DIM)
    row = lambda a: a.reshape(1, -1)
    return {
        "g_mix": row(g_mix[l]), "w_in": packed.astype(bf16), "g_qa": row(g_qa[l]),
        "w_q": wq.astype(bf16),
        "g_qn": _pad_lanes(row(g_qnorm[l]), 0), "g_kva": row(g_kva[l]),
        "w_k": _pad_lanes(wkv[:, :, :NOPE_DIM], 0).reshape(KV_RANK, MLA_HEADS * LANE).astype(bf16),
        "w_v": wv.astype(bf16), "w_vT": wv.T.astype(bf16),
        "g_kn": _pad_lanes(row(g_knorm[l]), 0),
        "w_conv": w_conv[l], "b_conv": row(b_conv[l]),
        "b_gate": _pad_lanes(row(jnp.concatenate([b_igate[l], b_fgate[l]])), 0),
        "g_mh": row(g_mhead[l]),
        "w_out_a": w_out[l][:MLA_HEADS * V_DIM].astype(bf16), "w_out_m": w_out[l][MLA_HEADS * V_DIM:].astype(bf16),
        "g_xattn": row(g_xattn[l]), "g_mem": row(g_mem[l]),
        "w_xq": w_xq[l].astype(bf16), "w_xk": w_xk[l].astype(bf16), "w_xv": w_xv[l].astype(bf16),
        "g_xq": row(g_xq[l]), "g_xk": row(g_xk[l]), "w_xo": w_xo[l].astype(bf16),
        "g_mlp": row(g_mlp[l]), "w_ff1": w_ff1[l].astype(bf16), "w_ff2": w_ff2[l].astype(bf16),
    }


def _unpack_state(c_new, n_new, m_new, conv_new):
    return (c_new, n_new, m_new[:, 0, GATE_F:GATE_F + ML_HEADS], conv_new[:, SUBLANE - (CONV_W - 1):, :])


def kernel(x_prompt, x_sample, cache_mla_ckv, cache_mla_krope, state_mlstm_C, state_mlstm_n, state_mlstm_m,
           state_mlstm_conv, cache_mem_k, cache_mem_v, mem_prompt, g_mix, w_in, g_qa, w_q_up, g_qnorm, g_kva,
           w_kv_up, g_knorm, w_conv, b_conv, b_igate, b_fgate, g_mhead, w_out, g_xattn, g_mem, w_xq, w_xk, w_xv,
           g_xq, g_xk, w_xo, g_mlp, w_ff1, w_ff2):
    depth = w_in.shape[0]
    bp, sp, _ = x_prompt.shape
    bs, ss, _ = x_sample.shape
    past = cache_mla_ckv.shape[2]
    n_mem = mem_prompt.shape[1]
    weights = (g_mix, w_in, g_qa, w_q_up, g_qnorm, g_kva, w_kv_up, g_knorm, w_conv, b_conv, b_igate, b_fgate,
               g_mhead, w_out, g_xattn, g_mem, w_xq, w_xk, w_xv, g_xq, g_xk, w_xo, g_mlp, w_ff1, w_ff2)

    tabs_p = _rope_tables(jnp.arange(sp))
    tabs_s = tuple(jnp.tile(t, (bs, 1)) for t in _rope_tables(past + jnp.arange(ss)))
    ml_chunk = min(ML_CHUNK, sp)
    cache_ckv = cache_mla_ckv.reshape(depth * bs, past, KV_RANK)
    cache_kr = _pad_lanes(cache_mla_krope, NOPE_DIM).reshape(depth * bs, past, LANE)
    cache_c = state_mlstm_C.reshape(depth * bs, ML_HEADS, ML_DH, ML_DH)
    cache_mk = cache_mem_k.reshape(depth * bs, n_mem, XA_HEADS, XA_DH)
    cache_mv = cache_mem_v.reshape(depth * bs, n_mem, XA_HEADS, XA_DH)

    xp, xs = x_prompt, x_sample.reshape(1, bs * ss, D_MODEL)
    p_out = [[] for _ in range(8)]
    s_out = [[] for _ in range(6)]
    for l in range(depth):
        W = _layer_weights(l, *weights)

        mem_k, mem_v = _memkv(mem_prompt, W)
        q, ckv, kr, mqk, mv, og, gate = _inproj(xp, tabs_p, W)
        k, vt = _kvup(ckv, kr, W, True)
        bound = jnp.max(jnp.abs(g_qnorm[l])) * jnp.max(jnp.abs(g_knorm[l])) * (QK_DIM * MLA_SCALE * LOG2E * 1.05)
        a = lax.cond(bound <= SAFE_LOG2_SCORE, functools.partial(_mla_prompt, bounded=True),
                     functools.partial(_mla_prompt, bounded=False), q, k, vt)
        zeros = lambda *s: jnp.zeros((bp,) + s, f32)
        hm, *state = _mlstm(mqk, mv, gate, zeros(SUBLANE, 2 * ML_WIDTH), zeros(ML_HEADS, ML_DH, ML_DH),
                            zeros(ML_HEADS, ML_DH), zeros(1, LANE), W, ml_chunk)
        xp = _tail(xp, a, hm, og, mem_k, mem_v, W)
        new = (ckv, kr[..., NOPE_DIM:QK_DIM]) + _unpack_state(*state) + (
            mem_k.reshape(bp, n_mem, XA_HEADS, XA_DH), mem_v.reshape(bp, n_mem, XA_HEADS, XA_DH))
        for lst, t in zip(p_out, new):
            lst.append(t)

        q, ckv, kr, mqk, mv, og, gate = _inproj(xs, tabs_s, W)
        per_stream = lambda t: t.reshape(bs, ss, t.shape[-1])
        ckv, kr = per_stream(ckv), per_stream(kr)
        kn, vn = _kvup(ckv, kr, W, False)
        kp, vp = _kvup(cache_ckv, cache_kr, W, False, first=l * bs, count=bs)
        a = _mla_sample(q, kp, vp, kn, vn)
        conv0 = jnp.pad(state_mlstm_conv[l], ((0, 0), (SUBLANE - (CONV_W - 1), 0), (0, 0)))
        m0 = _pad_lanes(state_mlstm_m[l], GATE_F).reshape(bs, 1, LANE)
        hm, *state = _mlstm(per_stream(mqk), per_stream(mv), per_stream(gate), conv0, cache_c,
                            state_mlstm_n[l], m0, W, ss, c_first=l * bs)
        flat = lambda t: t.reshape(1, bs * ss, t.shape[-1])
        xs = _outproj(xs, flat(a), flat(hm), og, W)
        xs = flat(_xattn(per_stream(xs), cache_mk, cache_mv, W, mem_first=l * bs))
        xs = _mlp(xs, W)
        new = (ckv, kr[..., NOPE_DIM:QK_DIM]) + _unpack_state(*state)
        for lst, t in zip(s_out, new):
            lst.append(t)

    outs_p = tuple(jnp.stack(t) for t in p_out)
    outs_s = tuple(jnp.stack(t) for t in s_out)
    return (xp, xs.reshape(bs, ss, D_MODEL)) + outs_p + outs_s
```

```python
import functools

import jax
import jax.numpy as jnp
import numpy as np
from jax import lax
from jax.experimental import pallas as pl
from jax.experimental.pallas import tpu as pltpu

f32 = jnp.float32
bf16 = jnp.bfloat16

D_MODEL = 1024
CHUNK = 64
EPS = 1e-6
MLA_HEADS = 8
Q_RANK = 256
KV_RANK = 128
NOPE_DIM = 64
ROPE_DIM = 32
QK_DIM = NOPE_DIM + ROPE_DIM
V_DIM = 64
ROPE_THETA = 10000.0
MLA_SCALE = QK_DIM ** -0.5
ML_HEADS = 4
ML_DH = 128
ML_WIDTH = ML_HEADS * ML_DH
CONV_W = 4
XA_HEADS = 4
XA_DH = D_MODEL // XA_HEADS
XA_SCALE = XA_DH ** -0.5
D_FF = 4 * D_MODEL

LANE = 128
SUBLANE = 8
MIB = 1024 * 1024

C_QA = 0
C_KVA = C_QA + Q_RANK
C_KR = C_KVA + KV_RANK
C_MQK = C_KR + LANE
C_MV = C_MQK + 2 * ML_WIDTH
C_MO = C_MV + ML_WIDTH
C_GATE = C_MO + ML_WIDTH
C_KR_PARTNER = C_GATE + LANE
IN_COLS_PACKED = C_KR_PARTNER + LANE
GATE_F = ML_HEADS

ROW_TILE = 512
ATT_TILE = 2048
ATT_BLOCK = 1024
ATT_SUB = 512
FAST_HEADS_PER_TRIP = 4
ML_CHUNK = 256

BF16_ROWS = 16
V_EXT = V_DIM + BF16_ROWS
LOG2E = 1.4426950408889634
SAFE_LOG2_SCORE = 64.0

NT_DIMS = (((1,), (1,)), ((), ()))


def _rms(x, g):
    return x * lax.rsqrt(jnp.mean(x * x, axis=-1, keepdims=True) + EPS) * g


def _dot(a, b):
    return jnp.dot(a, b, preferred_element_type=f32)


def _dot_nt(a, b):
    return lax.dot_general(a, b, NT_DIMS, preferred_element_type=f32)


def _sigmoid(x):
    return 1.0 / (1.0 + jnp.exp(-x))


def _const_spec(shape):
    nd = len(shape)
    return pl.BlockSpec(shape, lambda *_: (0,) * nd, pipeline_mode=pl.Buffered(1))


def _params(semantics, vmem_mib):
    return pltpu.CompilerParams(dimension_semantics=semantics, vmem_limit_bytes=vmem_mib * MIB)


def _keys(c, kr, wk_ref, gk_ref, k_ref):
    kn = _dot(c, wk_ref[...])
    g = gk_ref[...]
    for hd in range(MLA_HEADS):
        t = kn[:, hd * LANE:(hd + 1) * LANE] + kr
        ss = jnp.sum(t * t, axis=-1, keepdims=True) * (1.0 / QK_DIM)
        k_ref[0, hd] = (t * lax.rsqrt(ss + EPS) * g).astype(bf16)


def _keys_values_t(ckv, kr, wk_ref, wvt_ref, gk_ref, k_ref, vt_ref):
    c = ckv.astype(bf16)
    _keys(c, kr, wk_ref, gk_ref, k_ref)
    vt = _dot_nt(wvt_ref[...], c).astype(bf16)
    ones_tile = (lax.broadcasted_iota(jnp.int32, (BF16_ROWS, vt.shape[1]), 0) == 0).astype(bf16)
    for hd in range(MLA_HEADS):
        vt_ref[0, hd, 0:V_DIM, :] = vt[hd * V_DIM:(hd + 1) * V_DIM, :]
        vt_ref[0, hd, V_DIM:V_EXT, :] = ones_tile


def _inproj_kernel(x_ref, cos_ref, sin_ref, gmix_ref, win_ref, gqa_ref, wq_ref, gqn_ref, gkva_ref, *rest, with_kv):
    if with_kv:
        wk_ref, wvt_ref, gk_ref, q_ref, ckv_ref, kr_ref, mqk_ref, mv_ref, og_ref, gate_ref, k_ref, vt_ref = rest
    else:
        q_ref, ckv_ref, kr_ref, mqk_ref, mv_ref, og_ref, gate_ref = rest
    h = _rms(x_ref[0], gmix_ref[...]).astype(bf16)
    proj = lambda lo, hi: _dot(h, win_ref[:, lo:hi])
    cos, sin = cos_ref[...], sin_ref[...]
    q_lat = _rms(proj(C_QA, C_KVA), gqa_ref[...])
    qf = _dot(q_lat.astype(bf16), wq_ref[...])
    gq = gqn_ref[...] * (MLA_SCALE * LOG2E)
    for hd in range(MLA_HEADS):
        lanes = slice(hd * LANE, (hd + 1) * LANE)
        partner = slice((MLA_HEADS + hd) * LANE, (MLA_HEADS + hd + 1) * LANE)
        t = qf[:, lanes] * cos + qf[:, partner] * sin
        ss = jnp.sum(t * t, axis=-1, keepdims=True) * (1.0 / QK_DIM)
        q_ref[0, hd] = (t * lax.rsqrt(ss + EPS) * gq).astype(bf16)
    latent = proj(C_KVA, C_MQK)
    gate_and_partner = proj(C_GATE, IN_COLS_PACKED)
    ckv = _rms(latent[:, :KV_RANK], gkva_ref[...])
    kr = latent[:, KV_RANK:] * cos + gate_and_partner[:, LANE:] * sin
    ckv_ref[0] = ckv
    kr_ref[0] = kr
    if with_kv:
        _keys_values_t(ckv, kr, wk_ref, wvt_ref, gk_ref, k_ref, vt_ref)
    mqk_ref[0] = proj(C_MQK, C_MV)
    mv_ref[0] = proj(C_MV, C_MO)
    og_ref[0] = proj(C_MO, C_GATE)
    gate_ref[0] = gate_and_partner[:, :LANE]


def _inproj(x, tabs, W, with_kv):
    bv, sv, _ = x.shape
    tm = min(ROW_TILE, sv)
    row = lambda w: pl.BlockSpec((1, tm, w), lambda b, j: (b, j, 0))
    tab = pl.BlockSpec((tm, LANE), lambda b, j: (j, 0))
    heads = pl.BlockSpec((1, MLA_HEADS, tm, LANE), lambda b, j: (b, 0, j, 0))
    sds = jax.ShapeDtypeStruct
    in_specs = [row(D_MODEL), tab, tab,
                _const_spec((1, D_MODEL)), _const_spec((D_MODEL, IN_COLS_PACKED)),
                _const_spec((1, Q_RANK)), _const_spec((Q_RANK, 2 * MLA_HEADS * LANE)),
                _const_spec((1, LANE)), _const_spec((1, KV_RANK))]
    out_specs = [heads, row(KV_RANK), row(LANE), row(2 * ML_WIDTH), row(ML_WIDTH), row(ML_WIDTH), row(LANE)]
    out_shape = [sds((bv, MLA_HEADS, sv, LANE), bf16), sds((bv, sv, KV_RANK), f32), sds((bv, sv, LANE), f32),
                 sds((bv, sv, 2 * ML_WIDTH), f32), sds((bv, sv, ML_WIDTH), f32), sds((bv, sv, ML_WIDTH), f32),
                 sds((bv, sv, LANE), f32)]
    operands = [x, *tabs, W["g_mix"], W["w_in"], W["g_qa"], W["w_q"], W["g_qn"], W["g_kva"]]
    if with_kv:
        in_specs += [_const_spec((KV_RANK, MLA_HEADS * LANE)), _const_spec((MLA_HEADS * V_DIM, KV_RANK)),
                     _const_spec((1, LANE))]
        out_specs += [heads, pl.BlockSpec((1, MLA_HEADS, V_EXT, tm), lambda b, j: (b, 0, 0, j))]
        out_shape += [sds((bv, MLA_HEADS, sv, LANE), bf16), sds((bv, MLA_HEADS, V_EXT, sv), bf16)]
        operands += [W["w_k"], W["w_vT"], W["g_kn"]]
    return pl.pallas_call(
        functools.partial(_inproj_kernel, with_kv=with_kv),
        grid=(bv, sv // tm),
        in_specs=in_specs,
        out_specs=out_specs,
        out_shape=out_shape,
        compiler_params=_params(("parallel", "parallel"), 48),
        name="inproj_kv" if with_kv else "inproj",
    )(*operands)


def _kvup_kernel(ckv_ref, kr_ref, wk_ref, wv_ref, gk_ref, k_ref, v_ref):
    c = ckv_ref[0].astype(bf16)
    _keys(c, kr_ref[0], wk_ref, gk_ref, k_ref)
    v_ref[0] = _dot(c, wv_ref[...]).astype(bf16)


def _kvup(ckv, kr, W, first=0, count=None):
    _, sv, _ = ckv.shape
    bv = ckv.shape[0] if count is None else count
    tm = min(ROW_TILE, sv)
    hv = MLA_HEADS * V_DIM
    row_in = lambda w: pl.BlockSpec((1, tm, w), lambda b, j: (first + b, j, 0))
    return pl.pallas_call(
        _kvup_kernel,
        grid=(bv, sv // tm),
        in_specs=[row_in(KV_RANK), row_in(LANE), _const_spec((KV_RANK, MLA_HEADS * LANE)), _const_spec((KV_RANK, hv)),
                  _const_spec((1, LANE))],
        out_specs=[pl.BlockSpec((1, MLA_HEADS, tm, LANE), lambda b, j: (b, 0, j, 0)),
                   pl.BlockSpec((1, tm, hv), lambda b, j: (b, j, 0))],
        out_shape=[jax.ShapeDtypeStruct((bv, MLA_HEADS, sv, LANE), bf16), jax.ShapeDtypeStruct((bv, sv, hv), bf16)],
        compiler_params=_params(("parallel", "parallel"), 32),
        name="kvup",
    )(ckv, kr, W["w_k"], W["w_v"], W["g_kn"])


def _mla_prompt_kernel(qi_ref, ki_ref, q_ref, k_ref, vt_ref, o_ref, m_sc, acc_sc, *, tile, sub, bounded):
    p = pl.program_id(1)
    qi, ki = qi_ref[p], ki_ref[p]
    nsub = tile // sub
    blk = min(ATT_BLOCK, tile)
    blk_key_chunk = lax.broadcasted_iota(jnp.int32, (blk, 1), 0) // CHUNK
    blk_query_chunk = lax.broadcasted_iota(jnp.int32, (1, blk), 1) // CHUNK
    query_chunk = lax.broadcasted_iota(jnp.int32, (1, sub), 1) // CHUNK

    @pl.when(ki == 0)
    def _():
        m_sc[...] = jnp.full(m_sc.shape, -jnp.inf, f32)
        acc_sc[...] = jnp.zeros(acc_sc.shape, f32)

    def scores(hd, lo, hi, cols):
        return _dot_nt(k_ref[0, hd, lo:hi, :], q_ref[0, hd, cols, :])

    def pv(hd, lo, hi, pm):
        return _dot(vt_ref[0, hd, :, lo:hi], pm.astype(bf16))

    def finish(hd, cols, acc):
        row0 = hd * V_DIM if isinstance(hd, int) else pl.multiple_of(hd * V_DIM, V_DIM)
        o_ref[0, pl.ds(row0, V_DIM), cols] = (acc[:V_DIM] / acc[V_DIM:V_DIM + 1]).astype(o_ref.dtype)

    def fast_step(hd, diag):
        for qs in range(tile // blk):
            cols = slice(qs * blk, (qs + 1) * blk)
            acc = acc_sc[hd, :, cols]
            for kb in range(qs if diag else tile // blk):
                lo = kb * blk
                acc = acc + pv(hd, lo, lo + blk, jnp.exp2(scores(hd, lo, lo + blk, cols)))
            if diag:
                lo = qs * blk
                pm = jnp.where(blk_key_chunk <= blk_query_chunk, jnp.exp2(scores(hd, lo, lo + blk, cols)), 0.0)
                finish(hd, cols, acc + pv(hd, lo, lo + blk, pm))
            else:
                acc_sc[hd, :, cols] = acc

    def slow_step(hd, diag):
        for qs in range(nsub):
            cols = slice(qs * sub, (qs + 1) * sub)
            kv_len = (qs + 1) * sub if diag else tile
            s = scores(hd, 0, kv_len, cols)
            if diag:
                kc = lax.broadcasted_iota(jnp.int32, (kv_len, 1), 0) // CHUNK
                s = jnp.where(kc <= query_chunk + (qs * sub) // CHUNK, s, -jnp.inf)
            m_old = m_sc[hd, :, cols]
            m_new = jnp.maximum(m_old, jnp.max(s, axis=0, keepdims=True))
            acc = jnp.exp2(m_old - m_new) * acc_sc[hd, :, cols] + pv(hd, 0, kv_len, jnp.exp2(s - m_new))
            if diag:
                finish(hd, cols, acc)
            else:
                m_sc[hd, :, cols] = m_new
                acc_sc[hd, :, cols] = acc

    for diag, where in ((False, ki != qi), (True, ki == qi)):
        @pl.when(where)
        def _(diag=diag):
            if bounded:
                lax.fori_loop(0, MLA_HEADS, lambda hd, c: (fast_step(hd, diag), c)[1], 0, unroll=FAST_HEADS_PER_TRIP)
            else:
                lax.fori_loop(0, MLA_HEADS, lambda hd, c: (slow_step(hd, diag), c)[1], 0)


def _mla_prompt(q, k, vt, bounded):
    bv, _, sv, _ = q.shape
    tile = min(ATT_TILE, sv)
    sub = min(ATT_SUB, tile)
    nq = sv // tile
    pairs = [(i, j) for i in range(nq) for j in range(i + 1)]
    qi = jnp.asarray(np.array([a for a, _ in pairs], np.int32))
    ki = jnp.asarray(np.array([b for _, b in pairs], np.int32))
    hv = MLA_HEADS * V_DIM
    grid_spec = pltpu.PrefetchScalarGridSpec(
        num_scalar_prefetch=2,
        grid=(bv, len(pairs)),
        in_specs=[pl.BlockSpec((1, MLA_HEADS, tile, LANE), lambda b, p, qi, ki: (b, 0, qi[p], 0)),
                  pl.BlockSpec((1, MLA_HEADS, tile, LANE), lambda b, p, qi, ki: (b, 0, ki[p], 0)),
                  pl.BlockSpec((1, MLA_HEADS, V_EXT, tile), lambda b, p, qi, ki: (b, 0, 0, ki[p]))],
        out_specs=pl.BlockSpec((1, hv, tile), lambda b, p, qi, ki: (b, 0, qi[p])),
        scratch_shapes=[pltpu.VMEM((MLA_HEADS, 1, tile), f32), pltpu.VMEM((MLA_HEADS, V_EXT, tile), f32)],
    )
    return pl.pallas_call(
        functools.partial(_mla_prompt_kernel, tile=tile, sub=sub, bounded=bounded),
        grid_spec=grid_spec,
        out_shape=jax.ShapeDtypeStruct((bv, hv, sv), bf16),
        compiler_params=_params(("parallel", "arbitrary"), 56),
        name="mla_prompt" if bounded else "mla_prompt_running_max",
    )(qi, ki, q, k, vt)


def _mla_sample_kernel(q_ref, kp_ref, vp_ref, kn_ref, vn_ref, o_ref):
    vp = vp_ref[0]
    vn = vn_ref[0]
    lane_head = lax.broadcasted_iota(jnp.int32, (1, MLA_HEADS * V_DIM), 1) // V_DIM
    out = jnp.zeros(o_ref.shape[1:], f32)
    for hd in range(MLA_HEADS):
        qh = q_ref[0, hd]
        s1 = _dot_nt(qh, kp_ref[0, hd])
        s2 = _dot_nt(qh, kn_ref[0, hd])
        m = jnp.maximum(jnp.max(s1, axis=-1, keepdims=True), jnp.max(s2, axis=-1, keepdims=True))
        p1 = jnp.exp2(s1 - m)
        p2 = jnp.exp2(s2 - m)
        l = jnp.sum(p1, axis=-1, keepdims=True) + jnp.sum(p2, axis=-1, keepdims=True)
        sel = lane_head == hd
        o = _dot(p1.astype(bf16), jnp.where(sel, vp, 0)) + _dot(p2.astype(bf16), jnp.where(sel, vn, 0))
        out = out + o / l
    o_ref[0] = out.astype(o_ref.dtype)


def _mla_sample(q, kp, vp, kn, vn):
    bv, _, sv, _ = kn.shape
    past = kp.shape[2]
    hv = MLA_HEADS * V_DIM
    return pl.pallas_call(
        _mla_sample_kernel,
        grid=(bv,),
        in_specs=[pl.BlockSpec((1, MLA_HEADS, sv, LANE), lambda b: (0, 0, b, 0)),
                  pl.BlockSpec((1, MLA_HEADS, past, LANE), lambda b: (b, 0, 0, 0)),
                  pl.BlockSpec((1, past, hv), lambda b: (b, 0, 0)),
                  pl.BlockSpec((1, MLA_HEADS, sv, LANE), lambda b: (b, 0, 0, 0)),
                  pl.BlockSpec((1, sv, hv), lambda b: (b, 0, 0))],
        out_specs=pl.BlockSpec((1, sv, hv), lambda b: (b, 0, 0)),
        out_shape=jax.ShapeDtypeStruct((bv, sv, hv), bf16),
        compiler_params=_params(("parallel",), 32),
        name="mla_sample",
    )(q, kp, vp, kn, vn)


def _mlstm_kernel(mqk_ref, mv_ref, gate_ref, conv0_ref, c0_ref, n0_ref, m0_ref, wconv_ref, bconv_ref, bgate_ref,
                  h_ref, cout_ref, nout_ref, mout_ref, convout_ref, ext_sc, c_sc, n_sc, m_sc, *, chunk):
    step = pl.program_id(1)
    L = chunk

    @pl.when(step == 0)
    def _():
        ext_sc[0:SUBLANE, :] = conv0_ref[0]
        c_sc[...] = c0_ref[0]
        n_sc[...] = n0_ref[0]
        m_sc[...] = m0_ref[0]

    x = mqk_ref[0]
    ext_sc[SUBLANE:SUBLANE + L, :] = x
    y = bconv_ref[...] + x * wconv_ref[CONV_W - 1:CONV_W, :]
    for j in range(CONV_W - 1):
        off = SUBLANE - (CONV_W - 1) + j
        y = y + ext_sc[off:off + L, :] * wconv_ref[j:j + 1, :]
    ext_sc[0:SUBLANE, :] = x[L - SUBLANE:L, :]
    qk = y * _sigmoid(y)
    mq = qk[:, :ML_WIDTH]
    mk = qk[:, ML_WIDTH:] * (ML_DH ** -0.5)
    mv = mv_ref[0]

    g = gate_ref[0] + bgate_ref[...]
    ls = jnp.minimum(g, 0.0) - jnp.log1p(jnp.exp(-jnp.abs(g)))
    r_i = lax.broadcasted_iota(jnp.int32, (L, L), 0)
    c_i = lax.broadcasted_iota(jnp.int32, (L, L), 1)
    causal = c_i <= r_i
    tri = causal.astype(f32)
    bcum = jnp.dot(tri, ls, preferred_element_type=f32, precision=lax.Precision.HIGHEST)
    g_t = g.T[0:SUBLANE, :]
    bcum_t = lax.dot_general(ls.T[0:SUBLANE, :], tri, NT_DIMS, preferred_element_type=f32,
                             precision=lax.Precision.HIGHEST)

    lane = lax.broadcasted_iota(jnp.int32, (1, LANE), 1)
    sub8 = lax.broadcasted_iota(jnp.int32, (SUBLANE, 1), 0)
    last_row = lax.broadcasted_iota(jnp.int32, (L, 1), 0) == L - 1

    def col(t, idx):
        return jnp.sum(jnp.where(lane == idx, t, 0.0), axis=1, keepdims=True)

    def row(t, idx):
        return jnp.sum(jnp.where(sub8 == idx, t, 0.0), axis=0, keepdims=True)

    m_vec = m_sc[...]
    m_next = jnp.zeros_like(m_vec)
    for hd in range(ML_HEADS):
        hs = slice(hd * ML_DH, (hd + 1) * ML_DH)
        b_col, ig_col = col(bcum, GATE_F + hd), col(g, hd)
        b_row, ig_row = row(bcum_t, GATE_F + hd), row(g_t, hd)
        m_prev = col(m_vec, GATE_F + hd)
        inter = b_col + m_prev
        d = jnp.where(causal, b_col - b_row + ig_row, -jnp.inf)
        m_t = jnp.maximum(inter, jnp.max(d, axis=1, keepdims=True))
        w = jnp.exp(d - m_t)
        a_inter = jnp.exp(inter - m_t)
        qh, kh, vh = mq[:, hs], mk[:, hs], mv[:, hs]
        qb, vb = qh.astype(bf16), vh.astype(bf16)
        sqk = _dot_nt(qb, kh.astype(bf16)) * w
        c_h = c_sc[hd]
        n_h = n_sc[hd:hd + 1, :]
        num = a_inter * _dot(qb, c_h.astype(bf16)) + _dot(sqk.astype(bf16), vb)
        qn = a_inter * jnp.sum(qh * n_h, axis=1, keepdims=True) + jnp.sum(sqk, axis=1, keepdims=True)
        h_ref[0, :, hs] = num / jnp.maximum(jnp.abs(qn), jnp.exp(-m_t))
        b_last = jnp.sum(jnp.where(last_row, b_col, 0.0), axis=0, keepdims=True)
        m_end = jnp.sum(jnp.where(last_row, m_t, 0.0), axis=0, keepdims=True)
        decay = jnp.exp(b_last + m_prev - m_end)
        kw = kh * jnp.exp(b_last - b_col + ig_col - m_end)
        c_sc[hd] = decay * c_h + _dot(kw.T.astype(bf16), vb)
        n_sc[hd:hd + 1, :] = decay * n_h + jnp.sum(kw, axis=0, keepdims=True)
        m_next = m_next + jnp.where(lane == GATE_F + hd, m_end, 0.0)
    m_sc[...] = m_next

    @pl.when(step == pl.num_programs(1) - 1)
    def _():
        cout_ref[0] = c_sc[...]
        nout_ref[0] = n_sc[...]
        mout_ref[0] = m_sc[...]
        convout_ref[0] = ext_sc[0:SUBLANE, :]


def _mlstm(mqk, mv, gate, conv0, c0, n0, m0, W, chunk, c_first=0):
    bv, sv, _ = mqk.shape
    step = lambda w: pl.BlockSpec((1, chunk, w), lambda b, c: (b, c, 0))
    per_b = lambda *s: pl.BlockSpec((1,) + s, lambda b, c: (b,) + (0,) * len(s))
    c0_spec = pl.BlockSpec((1, ML_HEADS, ML_DH, ML_DH), lambda b, c: (c_first + b, 0, 0, 0))
    sds = jax.ShapeDtypeStruct
    return pl.pallas_call(
        functools.partial(_mlstm_kernel, chunk=chunk),
        grid=(bv, sv // chunk),
        in_specs=[step(2 * ML_WIDTH), step(ML_WIDTH), step(LANE),
                  per_b(SUBLANE, 2 * ML_WIDTH), c0_spec, per_b(ML_HEADS, ML_DH), per_b(1, LANE),
                  _const_spec((CONV_W, 2 * ML_WIDTH)), _const_spec((1, 2 * ML_WIDTH)), _const_spec((1, LANE))],
        out_specs=[step(ML_WIDTH), per_b(ML_HEADS, ML_DH, ML_DH), per_b(ML_HEADS, ML_DH), per_b(1, LANE),
                   per_b(SUBLANE, 2 * ML_WIDTH)],
        out_shape=[sds((bv, sv, ML_WIDTH), f32), sds((bv, ML_HEADS, ML_DH, ML_DH), f32), sds((bv, ML_HEADS, ML_DH), f32),
                   sds((bv, 1, LANE), f32), sds((bv, SUBLANE, 2 * ML_WIDTH), f32)],
        scratch_shapes=[pltpu.VMEM((SUBLANE + chunk, 2 * ML_WIDTH), f32), pltpu.VMEM((ML_HEADS, ML_DH, ML_DH), f32),
                        pltpu.VMEM((ML_HEADS, ML_DH), f32), pltpu.VMEM((1, LANE), f32)],
        compiler_params=_params(("parallel", "arbitrary"), 32),
        name="mlstm",
    )(mqk, mv, gate, conv0, c0, n0, m0, W["w_conv"], W["b_conv"], W["b_gate"])


def _outproj_rows(x, a, hm, og, gmh_ref, wa_ref, wm_ref):
    parts = []
    for hd in range(ML_HEADS):
        t = hm[:, hd * ML_DH:(hd + 1) * ML_DH]
        parts.append(t * lax.rsqrt(jnp.mean(t * t, axis=-1, keepdims=True) + EPS))
    hn = jnp.concatenate(parts, axis=1) * gmh_ref[...] * _sigmoid(og)
    return x + _dot(a, wa_ref[...]) + _dot(hn.astype(bf16), wm_ref[...])


def _outproj_kernel(x_ref, a_ref, hm_ref, og_ref, gmh_ref, wa_ref, wm_ref, o_ref):
    o_ref[0] = _outproj_rows(x_ref[0], a_ref[0], hm_ref[0], og_ref[0], gmh_ref, wa_ref, wm_ref)


def _outproj(x, a, hm, og, W):
    bv, sv, _ = x.shape
    tm = min(ROW_TILE, sv)
    hv = MLA_HEADS * V_DIM
    row = lambda w: pl.BlockSpec((1, tm, w), lambda b, j: (b, j, 0))
    return pl.pallas_call(
        _outproj_kernel,
        grid=(bv, sv // tm),
        in_specs=[row(D_MODEL), row(hv), row(ML_WIDTH), row(ML_WIDTH), _const_spec((1, ML_WIDTH)),
                  _const_spec((hv, D_MODEL)), _const_spec((ML_WIDTH, D_MODEL))],
        out_specs=row(D_MODEL),
        out_shape=jax.ShapeDtypeStruct(x.shape, f32),
        compiler_params=_params(("parallel", "parallel"), 32),
        name="outproj",
    )(x, a, hm, og, W["g_mh"], W["w_out_a"], W["w_out_m"])


def _xattn_rows(x, mk_heads, mv_heads, gx_ref, wq_ref, gxq_ref, wo_ref):
    qx = _dot(_rms(x, gx_ref[...]).astype(bf16), wq_ref[...])
    gxq = gxq_ref[...]
    outs = []
    for hd in range(XA_HEADS):
        t = _rms(qx[:, hd * XA_DH:(hd + 1) * XA_DH], gxq) * XA_SCALE
        s = _dot_nt(t.astype(bf16), mk_heads[hd])
        pm = jnp.exp(s - jnp.max(s, axis=-1, keepdims=True))
        outs.append(_dot(pm.astype(bf16), mv_heads[hd]) / jnp.sum(pm, axis=-1, keepdims=True))
    ox = jnp.concatenate(outs, axis=1)
    return x + _dot(ox.astype(bf16), wo_ref[...])


def _xattn_kernel(x_ref, mk_ref, mv_ref, gx_ref, wq_ref, gxq_ref, wo_ref, o_ref):
    mk_heads = [mk_ref[0, :, hd, :].astype(bf16) for hd in range(XA_HEADS)]
    mv_heads = [mv_ref[0, :, hd, :].astype(bf16) for hd in range(XA_HEADS)]
    o_ref[0] = _xattn_rows(x_ref[0], mk_heads, mv_heads, gx_ref, wq_ref, gxq_ref, wo_ref)


def _xattn(x, mem_k, mem_v, W, mem_first=0):
    bv, sv, _ = x.shape
    tm = min(ROW_TILE, sv)
    n_mem = mem_k.shape[1]
    row = pl.BlockSpec((1, tm, D_MODEL), lambda b, j: (b, j, 0))
    mem = pl.BlockSpec((1, n_mem, XA_HEADS, XA_DH), lambda b, j: (mem_first + b, 0, 0, 0))
    return pl.pallas_call(
        _xattn_kernel,
        grid=(bv, sv // tm),
        in_specs=[row, mem, mem, _const_spec((1, D_MODEL)), _const_spec((D_MODEL, D_MODEL)),
                  _const_spec((1, XA_DH)), _const_spec((D_MODEL, D_MODEL))],
        out_specs=row,
        out_shape=jax.ShapeDtypeStruct(x.shape, f32),
        compiler_params=_params(("parallel", "parallel"), 40),
        name="xattn",
    )(x, mem_k, mem_v, W["g_xattn"], W["w_xq"], W["g_xq"], W["w_xo"])


def _mlp_rows(x, g_ref, w1_ref, w2_ref):
    hf = _rms(x, g_ref[...]).astype(bf16)
    acc = x
    for c in range(D_FF // D_MODEL):
        cs = slice(c * D_MODEL, (c + 1) * D_MODEL)
        u = jnp.square(jnp.maximum(_dot(hf, w1_ref[:, cs]), 0.0))
        acc = acc + _dot(u.astype(bf16), w2_ref[cs, :])
    return acc


def _mlp_kernel(x_ref, g_ref, w1_ref, w2_ref, o_ref):
    o_ref[0] = _mlp_rows(x_ref[0], g_ref, w1_ref, w2_ref)


def _mlp(x, W):
    bv, sv, _ = x.shape
    tm = min(ROW_TILE, sv)
    row = pl.BlockSpec((1, tm, D_MODEL), lambda b, j: (b, j, 0))
    return pl.pallas_call(
        _mlp_kernel,
        grid=(bv, sv // tm),
        in_specs=[row, _const_spec((1, D_MODEL)), _const_spec((D_MODEL, D_FF)), _const_spec((D_FF, D_MODEL))],
        out_specs=row,
        out_shape=jax.ShapeDtypeStruct(x.shape, f32),
        compiler_params=_params(("parallel", "parallel"), 48),
        name="mlp",
    )(x, W["g_mlp"], W["w_ff1"], W["w_ff2"])


def _tail_kernel(x_ref, at_ref, hm_ref, og_ref, mk_ref, mv_ref, gmh_ref, wa_ref, wm_ref, gx_ref, wq_ref, gxq_ref,
                 wo_ref, gmlp_ref, w1_ref, w2_ref, o_ref):
    mk, mv = mk_ref[0].astype(bf16), mv_ref[0].astype(bf16)
    mk_heads = [mk[:, hd * XA_DH:(hd + 1) * XA_DH] for hd in range(XA_HEADS)]
    mv_heads = [mv[:, hd * XA_DH:(hd + 1) * XA_DH] for hd in range(XA_HEADS)]
    a = at_ref[0].astype(f32).T.astype(bf16)
    x = _outproj_rows(x_ref[0], a, hm_ref[0], og_ref[0], gmh_ref, wa_ref, wm_ref)
    x = _xattn_rows(x, mk_heads, mv_heads, gx_ref, wq_ref, gxq_ref, wo_ref)
    o_ref[0] = _mlp_rows(x, gmlp_ref, w1_ref, w2_ref)


def _tail(x, at, hm, og, mem_k, mem_v, W):
    bv, sv, _ = x.shape
    tm = min(ROW_TILE, sv)
    hv = MLA_HEADS * V_DIM
    n_mem = mem_k.shape[1]
    row = lambda w: pl.BlockSpec((1, tm, w), lambda b, j: (b, j, 0))
    mem = pl.BlockSpec((1, n_mem, D_MODEL), lambda b, j: (b, 0, 0))
    return pl.pallas_call(
        _tail_kernel,
        grid=(bv, sv // tm),
        in_specs=[row(D_MODEL), pl.BlockSpec((1, hv, tm), lambda b, j: (b, 0, j)), row(ML_WIDTH), row(ML_WIDTH), mem, mem,
                  _const_spec((1, ML_WIDTH)), _const_spec((hv, D_MODEL)), _const_spec((ML_WIDTH, D_MODEL)),
                  _const_spec((1, D_MODEL)), _const_spec((D_MODEL, D_MODEL)), _const_spec((1, XA_DH)),
                  _const_spec((D_MODEL, D_MODEL)),
                  _const_spec((1, D_MODEL)), _const_spec((D_MODEL, D_FF)), _const_spec((D_FF, D_MODEL))],
        out_specs=row(D_MODEL),
        out_shape=jax.ShapeDtypeStruct(x.shape, f32),
        compiler_params=_params(("parallel", "parallel"), 58),
        name="tail",
    )(x, at, hm, og, mem_k, mem_v, W["g_mh"], W["w_out_a"], W["w_out_m"], W["g_xattn"], W["w_xq"], W["g_xq"], W["w_xo"],
      W["g_mlp"], W["w_ff1"], W["w_ff2"])


def _memkv_kernel(mem_ref, gm_ref, wk_ref, wv_ref, gk_ref, k_ref, v_ref):
    hm = _rms(mem_ref[0], gm_ref[...]).astype(bf16)
    k = _dot(hm, wk_ref[...])
    gk = gk_ref[...]
    k_ref[0] = jnp.concatenate([_rms(k[:, hd * XA_DH:(hd + 1) * XA_DH], gk) for hd in range(XA_HEADS)], axis=1)
    v_ref[0] = _dot(hm, wv_ref[...])


def _memkv(mem, W):
    bv, n_mem, _ = mem.shape
    blk = pl.BlockSpec((1, n_mem, D_MODEL), lambda b: (b, 0, 0))
    return pl.pallas_call(
        _memkv_kernel,
        grid=(bv,),
        in_specs=[blk, _const_spec((1, D_MODEL)), _const_spec((D_MODEL, D_MODEL)), _const_spec((D_MODEL, D_MODEL)),
                  _const_spec((1, XA_DH))],
        out_specs=[blk, blk],
        out_shape=[jax.ShapeDtypeStruct(mem.shape, f32)] * 2,
        compiler_params=_params(("parallel",), 32),
        name="memkv",
    )(mem, W["g_mem"], W["w_xk"], W["w_xv"], W["g_xk"])


def _pad_lanes(a, lo, width=LANE):
    pad = [(0, 0)] * (a.ndim - 1) + [(lo, width - lo - a.shape[-1])]
    return jnp.pad(a, pad)


def _rope_tables(pos):
    half = ROPE_DIM // 2
    inv_freq = ROPE_THETA ** (-jnp.arange(half, dtype=f32) / half)
    ang = pos.astype(f32)[:, None] * inv_freq[None, :]
    cos, sin = jnp.cos(ang), jnp.sin(ang)
    n = pos.shape[0]
    ones = jnp.ones((n, NOPE_DIM), f32)
    tail = jnp.zeros((n, LANE - QK_DIM), f32)
    cos_t = jnp.concatenate([ones, cos, cos, tail], axis=1)
    sin_t = jnp.concatenate([0 * ones, sin, sin, tail], axis=1)
    return cos_t, sin_t


def _rope_partner(w):
    half = ROPE_DIM // 2
    x1, x2 = w[..., -ROPE_DIM:-half], w[..., -half:]
    return jnp.concatenate([jnp.zeros_like(w[..., :-ROPE_DIM]), -x2, x1], axis=-1)


def _layer_weights(l, g_mix, w_in, g_qa, w_q_up, g_qnorm, g_kva, w_kv_up, g_knorm, w_conv, b_conv, b_igate, b_fgate,
                   g_mhead, w_out, g_xattn, g_mem, w_xq, w_xk, w_xv, g_xq, g_xk, w_xo, g_mlp, w_ff1, w_ff2):
    wi = w_in[l]
    off_kr = Q_RANK + KV_RANK
    off_mqk = off_kr + ROPE_DIM
    off_mv = off_mqk + 2 * ML_WIDTH
    off_mi = off_mv + ML_WIDTH
    off_mo = off_mi + 2 * ML_HEADS
    w_kr = wi[:, off_kr:off_mqk]
    packed = jnp.concatenate([
        wi[:, :off_kr],
        _pad_lanes(w_kr, NOPE_DIM),
        wi[:, off_mqk:off_mi],
        wi[:, off_mo:],
        _pad_lanes(wi[:, off_mi:off_mo], 0),
        _pad_lanes(_rope_partner(w_kr), NOPE_DIM),
    ], axis=1)
    wq = w_q_up[l].reshape(Q_RANK, MLA_HEADS, QK_DIM)
    wq = jnp.concatenate([_pad_lanes(wq, 0).reshape(Q_RANK, MLA_HEADS * LANE),
                          _pad_lanes(_rope_partner(wq), 0).reshape(Q_RANK, MLA_HEADS * LANE)], axis=1)
    wkv = w_kv_up[l].reshape(KV_RANK, MLA_HEADS, NOPE_DIM + V_DIM)
    wv = wkv[:, :, NOPE_DIM:].reshape(KV_RANK, MLA_HEADS * V_DIM)
    row = lambda a: a.reshape(1, -1)
    return {
        "g_mix": row(g_mix[l]), "w_in": packed.astype(bf16), "g_qa": row(g_qa[l]),
        "w_q": wq.astype(bf16),
        "g_qn": _pad_lanes(row(g_qnorm[l]), 0), "g_kva": row(g_kva[l]),
        "w_k": _pad_lanes(wkv[:, :, :NOPE_DIM], 0).reshape(KV_RANK, MLA_HEADS * LANE).astype(bf16),
        "w_v": wv.astype(bf16), "w_vT": wv.T.astype(bf16),
        "g_kn": _pad_lanes(row(g_knorm[l]), 0),
        "w_conv": w_conv[l], "b_conv": row(b_conv[l]),
        "b_gate": _pad_lanes(row(jnp.concatenate([b_igate[l], b_fgate[l]])), 0),
        "g_mh": row(g_mhead[l]),
        "w_out_a": w_out[l][:MLA_HEADS * V_DIM].astype(bf16), "w_out_m": w_out[l][MLA_HEADS * V_DIM:].astype(bf16),
        "g_xattn": row(g_xattn[l]), "g_mem": row(g_mem[l]),
        "w_xq": w_xq[l].astype(bf16), "w_xk": w_xk[l].astype(bf16), "w_xv": w_xv[l].astype(bf16),
        "g_xq": row(g_xq[l]), "g_xk": row(g_xk[l]), "w_xo": w_xo[l].astype(bf16),
        "g_mlp": row(g_mlp[l]), "w_ff1": w_ff1[l].astype(bf16), "w_ff2": w_ff2[l].astype(bf16),
    }


def _unpack_state(c_new, n_new, m_new, conv_new):
    return (c_new, n_new, m_new[:, 0, GATE_F:GATE_F + ML_HEADS], conv_new[:, SUBLANE - (CONV_W - 1):, :])


def kernel(x_prompt, x_sample, cache_mla_ckv, cache_mla_krope, state_mlstm_C, state_mlstm_n, state_mlstm_m,
           state_mlstm_conv, cache_mem_k, cache_mem_v, mem_prompt, g_mix, w_in, g_qa, w_q_up, g_qnorm, g_kva,
           w_kv_up, g_knorm, w_conv, b_conv, b_igate, b_fgate, g_mhead, w_out, g_xattn, g_mem, w_xq, w_xk, w_xv,
           g_xq, g_xk, w_xo, g_mlp, w_ff1, w_ff2):
    depth = w_in.shape[0]
    bp, sp, _ = x_prompt.shape
    bs, ss, _ = x_sample.shape
    past = cache_mla_ckv.shape[2]
    n_mem = mem_prompt.shape[1]
    weights = (g_mix, w_in, g_qa, w_q_up, g_qnorm, g_kva, w_kv_up, g_knorm, w_conv, b_conv, b_igate, b_fgate,
               g_mhead, w_out, g_xattn, g_mem, w_xq, w_xk, w_xv, g_xq, g_xk, w_xo, g_mlp, w_ff1, w_ff2)

    tabs_p = _rope_tables(jnp.arange(sp))
    tabs_s = tuple(jnp.tile(t, (bs, 1)) for t in _rope_tables(past + jnp.arange(ss)))
    ml_chunk = min(ML_CHUNK, sp)
    cache_ckv = cache_mla_ckv.reshape(depth * bs, past, KV_RANK)
    cache_kr = _pad_lanes(cache_mla_krope, NOPE_DIM).reshape(depth * bs, past, LANE)
    cache_c = state_mlstm_C.reshape(depth * bs, ML_HEADS, ML_DH, ML_DH)
    cache_mk = cache_mem_k.reshape(depth * bs, n_mem, XA_HEADS, XA_DH)
    cache_mv = cache_mem_v.reshape(depth * bs, n_mem, XA_HEADS, XA_DH)

    xp, xs = x_prompt, x_sample.reshape(1, bs * ss, D_MODEL)
    p_out = [[] for _ in range(8)]
    s_out = [[] for _ in range(6)]
    for l in range(depth):
        W = _layer_weights(l, *weights)

        mem_k, mem_v = _memkv(mem_prompt, W)
        q, ckv, kr, mqk, mv, og, gate, k, vt = _inproj(xp, tabs_p, W, True)
        bound = jnp.max(jnp.abs(g_qnorm[l])) * jnp.max(jnp.abs(g_knorm[l])) * (QK_DIM * MLA_SCALE * LOG2E * 1.05)
        a = lax.cond(bound <= SAFE_LOG2_SCORE, functools.partial(_mla_prompt, bounded=True),
                     functools.partial(_mla_prompt, bounded=False), q, k, vt)
        zeros = lambda *s: jnp.zeros((bp,) + s, f32)
        hm, *state = _mlstm(mqk, mv, gate, zeros(SUBLANE, 2 * ML_WIDTH), zeros(ML_HEADS, ML_DH, ML_DH),
                            zeros(ML_HEADS, ML_DH), zeros(1, LANE), W, ml_chunk)
        xp = _tail(xp, a, hm, og, mem_k, mem_v, W)
        new = (ckv, kr[..., NOPE_DIM:QK_DIM]) + _unpack_state(*state) + (
            mem_k.reshape(bp, n_mem, XA_HEADS, XA_DH), mem_v.reshape(bp, n_mem, XA_HEADS, XA_DH))
        for lst, t in zip(p_out, new):
            lst.append(t)

        q, ckv, kr, mqk, mv, og, gate = _inproj(xs, tabs_s, W, False)
        per_stream = lambda t: t.reshape(bs, ss, t.shape[-1])
        ckv, kr = per_stream(ckv), per_stream(kr)
        kn, vn = _kvup(ckv, kr, W)
        kp, vp = _kvup(cache_ckv, cache_kr, W, first=l * bs, count=bs)
        a = _mla_sample(q, kp, vp, kn, vn)
        conv0 = jnp.pad(state_mlstm_conv[l], ((0, 0), (SUBLANE - (CONV_W - 1), 0), (0, 0)))
        m0 = _pad_lanes(state_mlstm_m[l], GATE_F).reshape(bs, 1, LANE)
        hm, *state = _mlstm(per_stream(mqk), per_stream(mv), per_stream(gate), conv0, cache_c,
                            state_mlstm_n[l], m0, W, ss, c_first=l * bs)
        flat = lambda t: t.reshape(1, bs * ss, t.shape[-1])
        xs = _outproj(xs, flat(a), flat(hm), og, W)
        xs = flat(_xattn(per_stream(xs), cache_mk, cache_mv, W, mem_first=l * bs))
        xs = _mlp(xs, W)
        new = (ckv, kr[..., NOPE_DIM:QK_DIM]) + _unpack_state(*state)
        for lst, t in zip(s_out, new):
            lst.append(t)

    outs_p = tuple(jnp.stack(t) for t in p_out)
    outs_s = tuple(jnp.stack(t) for t in s_out)
    return (xp, xs.reshape(bs, ss, D_MODEL)) + outs_p + outs_s
```

```python
import functools

import jax
import jax.numpy as jnp
import numpy as np
from jax import lax
from jax.experimental import pallas as pl
from jax.experimental.pallas import tpu as pltpu

f32 = jnp.float32
bf16 = jnp.bfloat16

D_MODEL = 1024
CHUNK = 64
EPS = 1e-6
MLA_HEADS = 8
Q_RANK = 256
KV_RANK = 128
NOPE_DIM = 64
ROPE_DIM = 32
QK_DIM = NOPE_DIM + ROPE_DIM
V_DIM = 64
ROPE_THETA = 10000.0
MLA_SCALE = QK_DIM ** -0.5
ML_HEADS = 4
ML_DH = 128
ML_WIDTH = ML_HEADS * ML_DH
CONV_W = 4
XA_HEADS = 4
XA_DH = D_MODEL // XA_HEADS
XA_SCALE = XA_DH ** -0.5
D_FF = 4 * D_MODEL

LANE = 128
SUBLANE = 8
MIB = 1024 * 1024

C_QA = 0
C_KVA = C_QA + Q_RANK
C_KR = C_KVA + KV_RANK
C_MQK = C_KR + LANE
C_MV = C_MQK + 2 * ML_WIDTH
C_MO = C_MV + ML_WIDTH
C_GATE = C_MO + ML_WIDTH
C_KR_PARTNER = C_GATE + LANE
IN_COLS_PACKED = C_KR_PARTNER + LANE
GATE_F = ML_HEADS

ROW_TILE = 512
ATT_TILE = 2048
ATT_BLOCK = 1024
ATT_SUB = 512
FAST_HEADS_PER_TRIP = 4
ML_CHUNK = 256

BF16_ROWS = 16
V_EXT = V_DIM + BF16_ROWS
LOG2E = 1.4426950408889634
SAFE_LOG2_SCORE = 64.0

NT_DIMS = (((1,), (1,)), ((), ()))


def _rms(x, g):
    return x * lax.rsqrt(jnp.mean(x * x, axis=-1, keepdims=True) + EPS) * g


def _dot(a, b):
    return jnp.dot(a, b, preferred_element_type=f32)


def _dot_nt(a, b):
    return lax.dot_general(a, b, NT_DIMS, preferred_element_type=f32)


def _sigmoid(x):
    return 1.0 / (1.0 + jnp.exp(-x))


def _const_spec(shape):
    nd = len(shape)
    return pl.BlockSpec(shape, lambda *_: (0,) * nd, pipeline_mode=pl.Buffered(1))


def _params(semantics, vmem_mib):
    return pltpu.CompilerParams(dimension_semantics=semantics, vmem_limit_bytes=vmem_mib * MIB)


def _keys(c, kr, wk_ref, gk_ref, k_ref):
    kn = _dot(c, wk_ref[...])
    g = gk_ref[...]
    for hd in range(MLA_HEADS):
        t = kn[:, hd * LANE:(hd + 1) * LANE] + kr
        ss = jnp.sum(t * t, axis=-1, keepdims=True) * (1.0 / QK_DIM)
        k_ref[0, hd] = (t * lax.rsqrt(ss + EPS) * g).astype(bf16)


def _keys_values_t(ckv, kr, wk_ref, wvt_ref, gk_ref, k_ref, vt_ref):
    c = ckv.astype(bf16)
    _keys(c, kr, wk_ref, gk_ref, k_ref)
    vt = _dot_nt(wvt_ref[...], c).astype(bf16)
    ones_tile = (lax.broadcasted_iota(jnp.int32, (BF16_ROWS, vt.shape[1]), 0) == 0).astype(bf16)
    for hd in range(MLA_HEADS):
        vt_ref[0, hd, 0:V_DIM, :] = vt[hd * V_DIM:(hd + 1) * V_DIM, :]
        vt_ref[0, hd, V_DIM:V_EXT, :] = ones_tile


def _inproj_kernel(x_ref, cos_ref, sin_ref, gmix_ref, win_ref, gqa_ref, wq_ref, gqn_ref, gkva_ref, *rest, with_kv):
    if with_kv:
        wk_ref, wvt_ref, gk_ref, q_ref, ckv_ref, kr_ref, mqk_ref, mv_ref, og_ref, gate_ref, k_ref, vt_ref = rest
    else:
        q_ref, ckv_ref, kr_ref, mqk_ref, mv_ref, og_ref, gate_ref = rest
    h = _rms(x_ref[0], gmix_ref[...]).astype(bf16)
    proj = lambda lo, hi: _dot(h, win_ref[:, lo:hi])
    cos, sin = cos_ref[...], sin_ref[...]
    q_lat = _rms(proj(C_QA, C_KVA), gqa_ref[...])
    qf = _dot(q_lat.astype(bf16), wq_ref[...])
    gq = gqn_ref[...] * (MLA_SCALE * LOG2E)
    for hd in range(MLA_HEADS):
        lanes = slice(hd * LANE, (hd + 1) * LANE)
        partner = slice((MLA_HEADS + hd) * LANE, (MLA_HEADS + hd + 1) * LANE)
        t = qf[:, lanes] * cos + qf[:, partner] * sin
        ss = jnp.sum(t * t, axis=-1, keepdims=True) * (1.0 / QK_DIM)
        q_ref[0, hd] = (t * lax.rsqrt(ss + EPS) * gq).astype(bf16)
    latent = proj(C_KVA, C_MQK)
    gate_and_partner = proj(C_GATE, IN_COLS_PACKED)
    ckv = _rms(latent[:, :KV_RANK], gkva_ref[...])
    kr = latent[:, KV_RANK:] * cos + gate_and_partner[:, LANE:] * sin
    ckv_ref[0] = ckv
    kr_ref[0] = kr
    if with_kv:
        _keys_values_t(ckv, kr, wk_ref, wvt_ref, gk_ref, k_ref, vt_ref)
    mqk_ref[0] = proj(C_MQK, C_MV)
    mv_ref[0] = proj(C_MV, C_MO)
    og_ref[0] = proj(C_MO, C_GATE)
    gate_ref[0] = gate_and_partner[:, :LANE]


def _inproj(x, tabs, W, with_kv):
    bv, sv, _ = x.shape
    tm = min(ROW_TILE, sv)
    row = lambda w: pl.BlockSpec((1, tm, w), lambda b, j: (b, j, 0))
    tab = pl.BlockSpec((tm, LANE), lambda b, j: (j, 0))
    heads = pl.BlockSpec((1, MLA_HEADS, tm, LANE), lambda b, j: (b, 0, j, 0))
    sds = jax.ShapeDtypeStruct
    in_specs = [row(D_MODEL), tab, tab,
                _const_spec((1, D_MODEL)), _const_spec((D_MODEL, IN_COLS_PACKED)),
                _const_spec((1, Q_RANK)), _const_spec((Q_RANK, 2 * MLA_HEADS * LANE)),
                _const_spec((1, LANE)), _const_spec((1, KV_RANK))]
    out_specs = [heads, row(KV_RANK), row(LANE), row(2 * ML_WIDTH), row(ML_WIDTH), row(ML_WIDTH), row(LANE)]
    out_shape = [sds((bv, MLA_HEADS, sv, LANE), bf16), sds((bv, sv, KV_RANK), f32), sds((bv, sv, LANE), f32),
                 sds((bv, sv, 2 * ML_WIDTH), f32), sds((bv, sv, ML_WIDTH), f32), sds((bv, sv, ML_WIDTH), f32),
                 sds((bv, sv, LANE), f32)]
    operands = [x, *tabs, W["g_mix"], W["w_in"], W["g_qa"], W["w_q"], W["g_qn"], W["g_kva"]]
    if with_kv:
        in_specs += [_const_spec((KV_RANK, MLA_HEADS * LANE)), _const_spec((MLA_HEADS * V_DIM, KV_RANK)),
                     _const_spec((1, LANE))]
        out_specs += [heads, pl.BlockSpec((1, MLA_HEADS, V_EXT, tm), lambda b, j: (b, 0, 0, j))]
        out_shape += [sds((bv, MLA_HEADS, sv, LANE), bf16), sds((bv, MLA_HEADS, V_EXT, sv), bf16)]
        operands += [W["w_k"], W["w_vT"], W["g_kn"]]
    return pl.pallas_call(
        functools.partial(_inproj_kernel, with_kv=with_kv),
        grid=(bv, sv // tm),
        in_specs=in_specs,
        out_specs=out_specs,
        out_shape=out_shape,
        compiler_params=_params(("parallel", "parallel"), 48),
        name="inproj_kv" if with_kv else "inproj",
    )(*operands)


def _kvup_kernel(ckv_ref, kr_ref, wk_ref, wv_ref, gk_ref, k_ref, v_ref):
    c = ckv_ref[0].astype(bf16)
    _keys(c, kr_ref[0], wk_ref, gk_ref, k_ref)
    v_ref[0] = _dot(c, wv_ref[...]).astype(bf16)


def _kvup(ckv, kr, W, first=0, count=None):
    _, sv, _ = ckv.shape
    bv = ckv.shape[0] if count is None else count
    tm = min(ROW_TILE, sv)
    hv = MLA_HEADS * V_DIM
    row_in = lambda w: pl.BlockSpec((1, tm, w), lambda b, j: (first + b, j, 0))
    return pl.pallas_call(
        _kvup_kernel,
        grid=(bv, sv // tm),
        in_specs=[row_in(KV_RANK), row_in(LANE), _const_spec((KV_RANK, MLA_HEADS * LANE)), _const_spec((KV_RANK, hv)),
                  _const_spec((1, LANE))],
        out_specs=[pl.BlockSpec((1, MLA_HEADS, tm, LANE), lambda b, j: (b, 0, j, 0)),
                   pl.BlockSpec((1, tm, hv), lambda b, j: (b, j, 0))],
        out_shape=[jax.ShapeDtypeStruct((bv, MLA_HEADS, sv, LANE), bf16), jax.ShapeDtypeStruct((bv, sv, hv), bf16)],
        compiler_params=_params(("parallel", "parallel"), 32),
        name="kvup",
    )(ckv, kr, W["w_k"], W["w_v"], W["g_kn"])


def _mla_prompt_kernel(qi_ref, ki_ref, q_ref, k_ref, vt_ref, o_ref, m_sc, acc_sc, *, tile, sub, bounded):
    p = pl.program_id(1)
    qi, ki = qi_ref[p], ki_ref[p]
    nsub = tile // sub
    blk = min(ATT_BLOCK, tile)
    blk_key_chunk = lax.broadcasted_iota(jnp.int32, (blk, 1), 0) // CHUNK
    blk_query_chunk = lax.broadcasted_iota(jnp.int32, (1, blk), 1) // CHUNK
    query_chunk = lax.broadcasted_iota(jnp.int32, (1, sub), 1) // CHUNK

    @pl.when(ki == 0)
    def _():
        m_sc[...] = jnp.full(m_sc.shape, -jnp.inf, f32)
        acc_sc[...] = jnp.zeros(acc_sc.shape, f32)

    def scores(hd, lo, hi, cols):
        return _dot_nt(k_ref[0, hd, lo:hi, :], q_ref[0, hd, cols, :])

    def pv(hd, lo, hi, pm):
        return _dot(vt_ref[0, hd, :, lo:hi], pm.astype(bf16))

    def finish(hd, cols, acc):
        row0 = hd * V_DIM if isinstance(hd, int) else pl.multiple_of(hd * V_DIM, V_DIM)
        o_ref[0, pl.ds(row0, V_DIM), cols] = (acc[:V_DIM] / acc[V_DIM:V_DIM + 1]).astype(o_ref.dtype)

    def fast_step(hd, diag):
        for qs in range(tile // blk):
            cols = slice(qs * blk, (qs + 1) * blk)
            acc = acc_sc[hd, :, cols]
            for kb in range(qs if diag else tile // blk):
                lo = kb * blk
                acc = acc + pv(hd, lo, lo + blk, jnp.exp2(scores(hd, lo, lo + blk, cols)))
            if diag:
                lo = qs * blk
                pm = jnp.where(blk_key_chunk <= blk_query_chunk, jnp.exp2(scores(hd, lo, lo + blk, cols)), 0.0)
                finish(hd, cols, acc + pv(hd, lo, lo + blk, pm))
            else:
                acc_sc[hd, :, cols] = acc

    def slow_step(hd, diag):
        for qs in range(nsub):
            cols = slice(qs * sub, (qs + 1) * sub)
            kv_len = (qs + 1) * sub if diag else tile
            s = scores(hd, 0, kv_len, cols)
            if diag:
                kc = lax.broadcasted_iota(jnp.int32, (kv_len, 1), 0) // CHUNK
                s = jnp.where(kc <= query_chunk + (qs * sub) // CHUNK, s, -jnp.inf)
            m_old = m_sc[hd, :, cols]
            m_new = jnp.maximum(m_old, jnp.max(s, axis=0, keepdims=True))
            acc = jnp.exp2(m_old - m_new) * acc_sc[hd, :, cols] + pv(hd, 0, kv_len, jnp.exp2(s - m_new))
            if diag:
                finish(hd, cols, acc)
            else:
                m_sc[hd, :, cols] = m_new
                acc_sc[hd, :, cols] = acc

    for diag, where in ((False, ki != qi), (True, ki == qi)):
        @pl.when(where)
        def _(diag=diag):
            if bounded:
                lax.fori_loop(0, MLA_HEADS, lambda hd, c: (fast_step(hd, diag), c)[1], 0, unroll=FAST_HEADS_PER_TRIP)
            else:
                lax.fori_loop(0, MLA_HEADS, lambda hd, c: (slow_step(hd, diag), c)[1], 0)


def _mla_prompt(q, k, vt, bounded):
    bv, _, sv, _ = q.shape
    tile = min(ATT_TILE, sv)
    sub = min(ATT_SUB, tile)
    nq = sv // tile
    pairs = [(i, j) for i in range(nq) for j in range(i + 1)]
    qi = jnp.asarray(np.array([a for a, _ in pairs], np.int32))
    ki = jnp.asarray(np.array([b for _, b in pairs], np.int32))
    hv = MLA_HEADS * V_DIM
    grid_spec = pltpu.PrefetchScalarGridSpec(
        num_scalar_prefetch=2,
        grid=(bv, len(pairs)),
        in_specs=[pl.BlockSpec((1, MLA_HEADS, tile, LANE), lambda b, p, qi, ki: (b, 0, qi[p], 0)),
                  pl.BlockSpec((1, MLA_HEADS, tile, LANE), lambda b, p, qi, ki: (b, 0, ki[p], 0)),
                  pl.BlockSpec((1, MLA_HEADS, V_EXT, tile), lambda b, p, qi, ki: (b, 0, 0, ki[p]))],
        out_specs=pl.BlockSpec((1, hv, tile), lambda b, p, qi, ki: (b, 0, qi[p])),
        scratch_shapes=[pltpu.VMEM((MLA_HEADS, 1, tile), f32), pltpu.VMEM((MLA_HEADS, V_EXT, tile), f32)],
    )
    return pl.pallas_call(
        functools.partial(_mla_prompt_kernel, tile=tile, sub=sub, bounded=bounded),
        grid_spec=grid_spec,
        out_shape=jax.ShapeDtypeStruct((bv, hv, sv), bf16),
        compiler_params=_params(("parallel", "arbitrary"), 56),
        name="mla_prompt" if bounded else "mla_prompt_running_max",
    )(qi, ki, q, k, vt)


def _mla_sample_kernel(q_ref, kp_ref, vp_ref, kn_ref, vn_ref, o_ref):
    outs = []
    for hd in range(MLA_HEADS):
        qh = q_ref[0, hd]
        vs = slice(hd * V_DIM, (hd + 1) * V_DIM)
        s1 = _dot_nt(qh, kp_ref[0, hd])
        s2 = _dot_nt(qh, kn_ref[0, hd])
        m = jnp.maximum(jnp.max(s1, axis=-1, keepdims=True), jnp.max(s2, axis=-1, keepdims=True))
        p1 = jnp.exp2(s1 - m)
        p2 = jnp.exp2(s2 - m)
        l = jnp.sum(p1, axis=-1, keepdims=True) + jnp.sum(p2, axis=-1, keepdims=True)
        o = _dot(p1.astype(bf16), vp_ref[0, :, vs]) + _dot(p2.astype(bf16), vn_ref[0, :, vs])
        outs.append(o / l)
    o_ref[0] = jnp.concatenate(outs, axis=1).astype(o_ref.dtype)


def _mla_sample(q, kp, vp, kn, vn):
    bv, _, sv, _ = kn.shape
    past = kp.shape[2]
    hv = MLA_HEADS * V_DIM
    return pl.pallas_call(
        _mla_sample_kernel,
        grid=(bv,),
        in_specs=[pl.BlockSpec((1, MLA_HEADS, sv, LANE), lambda b: (0, 0, b, 0)),
                  pl.BlockSpec((1, MLA_HEADS, past, LANE), lambda b: (b, 0, 0, 0)),
                  pl.BlockSpec((1, past, hv), lambda b: (b, 0, 0)),
                  pl.BlockSpec((1, MLA_HEADS, sv, LANE), lambda b: (b, 0, 0, 0)),
                  pl.BlockSpec((1, sv, hv), lambda b: (b, 0, 0))],
        out_specs=pl.BlockSpec((1, sv, hv), lambda b: (b, 0, 0)),
        out_shape=jax.ShapeDtypeStruct((bv, sv, hv), bf16),
        compiler_params=_params(("parallel",), 32),
        name="mla_sample",
    )(q, kp, vp, kn, vn)


def _mlstm_kernel(mqk_ref, mv_ref, gate_ref, conv0_ref, c0_ref, n0_ref, m0_ref, wconv_ref, bconv_ref, bgate_ref,
                  h_ref, cout_ref, nout_ref, mout_ref, convout_ref, ext_sc, c_sc, n_sc, m_sc, *, chunk):
    step = pl.program_id(1)
    L = chunk

    @pl.when(step == 0)
    def _():
        ext_sc[0:SUBLANE, :] = conv0_ref[0]
        c_sc[...] = c0_ref[0]
        n_sc[...] = n0_ref[0]
        m_sc[...] = m0_ref[0]

    x = mqk_ref[0]
    ext_sc[SUBLANE:SUBLANE + L, :] = x
    y = bconv_ref[...] + x * wconv_ref[CONV_W - 1:CONV_W, :]
    for j in range(CONV_W - 1):
        off = SUBLANE - (CONV_W - 1) + j
        y = y + ext_sc[off:off + L, :] * wconv_ref[j:j + 1, :]
    ext_sc[0:SUBLANE, :] = x[L - SUBLANE:L, :]
    qk = y * _sigmoid(y)
    mq = qk[:, :ML_WIDTH]
    mk = qk[:, ML_WIDTH:] * (ML_DH ** -0.5)
    mv = mv_ref[0]

    g = gate_ref[0] + bgate_ref[...]
    ls = jnp.minimum(g, 0.0) - jnp.log1p(jnp.exp(-jnp.abs(g)))
    r_i = lax.broadcasted_iota(jnp.int32, (L, L), 0)
    c_i = lax.broadcasted_iota(jnp.int32, (L, L), 1)
    causal = c_i <= r_i
    tri = causal.astype(f32)
    bcum = jnp.dot(tri, ls, preferred_element_type=f32, precision=lax.Precision.HIGHEST)
    g_t = g.T[0:SUBLANE, :]
    bcum_t = lax.dot_general(ls.T[0:SUBLANE, :], tri, NT_DIMS, preferred_element_type=f32,
                             precision=lax.Precision.HIGHEST)

    lane = lax.broadcasted_iota(jnp.int32, (1, LANE), 1)
    sub8 = lax.broadcasted_iota(jnp.int32, (SUBLANE, 1), 0)
    last_row = lax.broadcasted_iota(jnp.int32, (L, 1), 0) == L - 1

    def col(t, idx):
        return jnp.sum(jnp.where(lane == idx, t, 0.0), axis=1, keepdims=True)

    def row(t, idx):
        return jnp.sum(jnp.where(sub8 == idx, t, 0.0), axis=0, keepdims=True)

    m_vec = m_sc[...]
    m_next = jnp.zeros_like(m_vec)
    for hd in range(ML_HEADS):
        hs = slice(hd * ML_DH, (hd + 1) * ML_DH)
        b_col, ig_col = col(bcum, GATE_F + hd), col(g, hd)
        b_row, ig_row = row(bcum_t, GATE_F + hd), row(g_t, hd)
        m_prev = col(m_vec, GATE_F + hd)
        inter = b_col + m_prev
        d = jnp.where(causal, b_col - b_row + ig_row, -jnp.inf)
        m_t = jnp.maximum(inter, jnp.max(d, axis=1, keepdims=True))
        w = jnp.exp(d - m_t)
        a_inter = jnp.exp(inter - m_t)
        qh, kh, vh = mq[:, hs], mk[:, hs], mv[:, hs]
        qb, vb = qh.astype(bf16), vh.astype(bf16)
        sqk = _dot_nt(qb, kh.astype(bf16)) * w
        c_h = c_sc[hd]
        n_h = n_sc[hd:hd + 1, :]
        num = a_inter * _dot(qb, c_h.astype(bf16)) + _dot(sqk.astype(bf16), vb)
        qn = a_inter * jnp.sum(qh * n_h, axis=1, keepdims=True) + jnp.sum(sqk, axis=1, keepdims=True)
        h_ref[0, :, hs] = num / jnp.maximum(jnp.abs(qn), jnp.exp(-m_t))
        b_last = jnp.sum(jnp.where(last_row, b_col, 0.0), axis=0, keepdims=True)
        m_end = jnp.sum(jnp.where(last_row, m_t, 0.0), axis=0, keepdims=True)
        decay = jnp.exp(b_last + m_prev - m_end)
        kw = kh * jnp.exp(b_last - b_col + ig_col - m_end)
        c_sc[hd] = decay * c_h + _dot(kw.T.astype(bf16), vb)
        n_sc[hd:hd + 1, :] = decay * n_h + jnp.sum(kw, axis=0, keepdims=True)
        m_next = m_next + jnp.where(lane == GATE_F + hd, m_end, 0.0)
    m_sc[...] = m_next

    @pl.when(step == pl.num_programs(1) - 1)
    def _():
        cout_ref[0] = c_sc[...]
        nout_ref[0] = n_sc[...]
        mout_ref[0] = m_sc[...]
        convout_ref[0] = ext_sc[0:SUBLANE, :]


def _mlstm(mqk, mv, gate, conv0, c0, n0, m0, W, chunk, c_first=0):
    bv, sv, _ = mqk.shape
    step = lambda w: pl.BlockSpec((1, chunk, w), lambda b, c: (b, c, 0))
    per_b = lambda *s: pl.BlockSpec((1,) + s, lambda b, c: (b,) + (0,) * len(s))
    c0_spec = pl.BlockSpec((1, ML_HEADS, ML_DH, ML_DH), lambda b, c: (c_first + b, 0, 0, 0))
    sds = jax.ShapeDtypeStruct
    return pl.pallas_call(
        functools.partial(_mlstm_kernel, chunk=chunk),
        grid=(bv, sv // chunk),
        in_specs=[step(2 * ML_WIDTH), step(ML_WIDTH), step(LANE),
                  per_b(SUBLANE, 2 * ML_WIDTH), c0_spec, per_b(ML_HEADS, ML_DH), per_b(1, LANE),
                  _const_spec((CONV_W, 2 * ML_WIDTH)), _const_spec((1, 2 * ML_WIDTH)), _const_spec((1, LANE))],
        out_specs=[step(ML_WIDTH), per_b(ML_HEADS, ML_DH, ML_DH), per_b(ML_HEADS, ML_DH), per_b(1, LANE),
                   per_b(SUBLANE, 2 * ML_WIDTH)],
        out_shape=[sds((bv, sv, ML_WIDTH), f32), sds((bv, ML_HEADS, ML_DH, ML_DH), f32), sds((bv, ML_HEADS, ML_DH), f32),
                   sds((bv, 1, LANE), f32), sds((bv, SUBLANE, 2 * ML_WIDTH), f32)],
        scratch_shapes=[pltpu.VMEM((SUBLANE + chunk, 2 * ML_WIDTH), f32), pltpu.VMEM((ML_HEADS, ML_DH, ML_DH), f32),
                        pltpu.VMEM((ML_HEADS, ML_DH), f32), pltpu.VMEM((1, LANE), f32)],
        compiler_params=_params(("parallel", "arbitrary"), 32),
        name="mlstm",
    )(mqk, mv, gate, conv0, c0, n0, m0, W["w_conv"], W["b_conv"], W["b_gate"])


def _outproj_rows(x, a, hm, og, gmh_ref, wa_ref, wm_ref):
    parts = []
    for hd in range(ML_HEADS):
        t = hm[:, hd * ML_DH:(hd + 1) * ML_DH]
        parts.append(t * lax.rsqrt(jnp.mean(t * t, axis=-1, keepdims=True) + EPS))
    hn = jnp.concatenate(parts, axis=1) * gmh_ref[...] * _sigmoid(og)
    return x + _dot(a, wa_ref[...]) + _dot(hn.astype(bf16), wm_ref[...])


def _outproj_kernel(x_ref, a_ref, hm_ref, og_ref, gmh_ref, wa_ref, wm_ref, o_ref):
    o_ref[0] = _outproj_rows(x_ref[0], a_ref[0], hm_ref[0], og_ref[0], gmh_ref, wa_ref, wm_ref)


def _outproj(x, a, hm, og, W):
    bv, sv, _ = x.shape
    tm = min(ROW_TILE, sv)
    hv = MLA_HEADS * V_DIM
    row = lambda w: pl.BlockSpec((1, tm, w), lambda b, j: (b, j, 0))
    return pl.pallas_call(
        _outproj_kernel,
        grid=(bv, sv // tm),
        in_specs=[row(D_MODEL), row(hv), row(ML_WIDTH), row(ML_WIDTH), _const_spec((1, ML_WIDTH)),
                  _const_spec((hv, D_MODEL)), _const_spec((ML_WIDTH, D_MODEL))],
        out_specs=row(D_MODEL),
        out_shape=jax.ShapeDtypeStruct(x.shape, f32),
        compiler_params=_params(("parallel", "parallel"), 32),
        name="outproj",
    )(x, a, hm, og, W["g_mh"], W["w_out_a"], W["w_out_m"])


def _xattn_rows(x, mk_heads, mv_heads, gx_ref, wq_ref, gxq_ref, wo_ref):
    qx = _dot(_rms(x, gx_ref[...]).astype(bf16), wq_ref[...])
    gxq = gxq_ref[...]
    outs = []
    for hd in range(XA_HEADS):
        t = _rms(qx[:, hd * XA_DH:(hd + 1) * XA_DH], gxq) * XA_SCALE
        s = _dot_nt(t.astype(bf16), mk_heads[hd])
        pm = jnp.exp(s - jnp.max(s, axis=-1, keepdims=True))
        outs.append(_dot(pm.astype(bf16), mv_heads[hd]) / jnp.sum(pm, axis=-1, keepdims=True))
    ox = jnp.concatenate(outs, axis=1)
    return x + _dot(ox.astype(bf16), wo_ref[...])


def _xattn_kernel(x_ref, mk_ref, mv_ref, gx_ref, wq_ref, gxq_ref, wo_ref, o_ref):
    mk_heads = [mk_ref[0, :, hd, :].astype(bf16) for hd in range(XA_HEADS)]
    mv_heads = [mv_ref[0, :, hd, :].astype(bf16) for hd in range(XA_HEADS)]
    o_ref[0] = _xattn_rows(x_ref[0], mk_heads, mv_heads, gx_ref, wq_ref, gxq_ref, wo_ref)


def _xattn(x, mem_k, mem_v, W, mem_first=0):
    bv, sv, _ = x.shape
    tm = min(ROW_TILE, sv)
    n_mem = mem_k.shape[1]
    row = pl.BlockSpec((1, tm, D_MODEL), lambda b, j: (b, j, 0))
    mem = pl.BlockSpec((1, n_mem, XA_HEADS, XA_DH), lambda b, j: (mem_first + b, 0, 0, 0))
    return pl.pallas_call(
        _xattn_kernel,
        grid=(bv, sv // tm),
        in_specs=[row, mem, mem, _const_spec((1, D_MODEL)), _const_spec((D_MODEL, D_MODEL)),
                  _const_spec((1, XA_DH)), _const_spec((D_MODEL, D_MODEL))],
        out_specs=row,
        out_shape=jax.ShapeDtypeStruct(x.shape, f32),
        compiler_params=_params(("parallel", "parallel"), 40),
        name="xattn",
    )(x, mem_k, mem_v, W["g_xattn"], W["w_xq"], W["g_xq"], W["w_xo"])


def _mlp_rows(x, g_ref, w1_ref, w2_ref):
    hf = _rms(x, g_ref[...]).astype(bf16)
    acc = x
    for c in range(D_FF // D_MODEL):
        cs = slice(c * D_MODEL, (c + 1) * D_MODEL)
        u = jnp.square(jnp.maximum(_dot(hf, w1_ref[:, cs]), 0.0))
        acc = acc + _dot(u.astype(bf16), w2_ref[cs, :])
    return acc


def _mlp_kernel(x_ref, g_ref, w1_ref, w2_ref, o_ref):
    o_ref[0] = _mlp_rows(x_ref[0], g_ref, w1_ref, w2_ref)


def _mlp(x, W):
    bv, sv, _ = x.shape
    tm = min(ROW_TILE, sv)
    row = pl.BlockSpec((1, tm, D_MODEL), lambda b, j: (b, j, 0))
    return pl.pallas_call(
        _mlp_kernel,
        grid=(bv, sv // tm),
        in_specs=[row, _const_spec((1, D_MODEL)), _const_spec((D_MODEL, D_FF)), _const_spec((D_FF, D_MODEL))],
        out_specs=row,
        out_shape=jax.ShapeDtypeStruct(x.shape, f32),
        compiler_params=_params(("parallel", "parallel"), 48),
        name="mlp",
    )(x, W["g_mlp"], W["w_ff1"], W["w_ff2"])


def _tail_kernel(x_ref, at_ref, hm_ref, og_ref, mk_ref, mv_ref, gmh_ref, wa_ref, wm_ref, gx_ref, wq_ref, gxq_ref,
                 wo_ref, gmlp_ref, w1_ref, w2_ref, o_ref):
    mk, mv = mk_ref[0].astype(bf16), mv_ref[0].astype(bf16)
    mk_heads = [mk[:, hd * XA_DH:(hd + 1) * XA_DH] for hd in range(XA_HEADS)]
    mv_heads = [mv[:, hd * XA_DH:(hd + 1) * XA_DH] for hd in range(XA_HEADS)]
    a = at_ref[0].astype(f32).T.astype(bf16)
    x = _outproj_rows(x_ref[0], a, hm_ref[0], og_ref[0], gmh_ref, wa_ref, wm_ref)
    x = _xattn_rows(x, mk_heads, mv_heads, gx_ref, wq_ref, gxq_ref, wo_ref)
    o_ref[0] = _mlp_rows(x, gmlp_ref, w1_ref, w2_ref)


def _tail(x, at, hm, og, mem_k, mem_v, W):
    bv, sv, _ = x.shape
    tm = min(ROW_TILE, sv)
    hv = MLA_HEADS * V_DIM
    n_mem = mem_k.shape[1]
    row = lambda w: pl.BlockSpec((1, tm, w), lambda b, j: (b, j, 0))
    mem = pl.BlockSpec((1, n_mem, D_MODEL), lambda b, j: (b, 0, 0))
    return pl.pallas_call(
        _tail_kernel,
        grid=(bv, sv // tm),
        in_specs=[row(D_MODEL), pl.BlockSpec((1, hv, tm), lambda b, j: (b, 0, j)), row(ML_WIDTH), row(ML_WIDTH), mem, mem,
                  _const_spec((1, ML_WIDTH)), _const_spec((hv, D_MODEL)), _const_spec((ML_WIDTH, D_MODEL)),
                  _const_spec((1, D_MODEL)), _const_spec((D_MODEL, D_MODEL)), _const_spec((1, XA_DH)),
                  _const_spec((D_MODEL, D_MODEL)),
                  _const_spec((1, D_MODEL)), _const_spec((D_MODEL, D_FF)), _const_spec((D_FF, D_MODEL))],
        out_specs=row(D_MODEL),
        out_shape=jax.ShapeDtypeStruct(x.shape, f32),
        compiler_params=_params(("parallel", "parallel"), 58),
        name="tail",
    )(x, at, hm, og, mem_k, mem_v, W["g_mh"], W["w_out_a"], W["w_out_m"], W["g_xattn"], W["w_xq"], W["g_xq"], W["w_xo"],
      W["g_mlp"], W["w_ff1"], W["w_ff2"])


def _memkv_kernel(mem_ref, gm_ref, wk_ref, wv_ref, gk_ref, k_ref, v_ref):
    hm = _rms(mem_ref[0], gm_ref[...]).astype(bf16)
    k = _dot(hm, wk_ref[...])
    gk = gk_ref[...]
    k_ref[0] = jnp.concatenate([_rms(k[:, hd * XA_DH:(hd + 1) * XA_DH], gk) for hd in range(XA_HEADS)], axis=1)
    v_ref[0] = _dot(hm, wv_ref[...])


def _memkv(mem, W):
    bv, n_mem, _ = mem.shape
    blk = pl.BlockSpec((1, n_mem, D_MODEL), lambda b: (b, 0, 0))
    return pl.pallas_call(
        _memkv_kernel,
        grid=(bv,),
        in_specs=[blk, _const_spec((1, D_MODEL)), _const_spec((D_MODEL, D_MODEL)), _const_spec((D_MODEL, D_MODEL)),
                  _const_spec((1, XA_DH))],
        out_specs=[blk, blk],
        out_shape=[jax.ShapeDtypeStruct(mem.shape, f32)] * 2,
        compiler_params=_params(("parallel",), 32),
        name="memkv",
    )(mem, W["g_mem"], W["w_xk"], W["w_xv"], W["g_xk"])


def _pad_lanes(a, lo, width=LANE):
    pad = [(0, 0)] * (a.ndim - 1) + [(lo, width - lo - a.shape[-1])]
    return jnp.pad(a, pad)


def _rope_tables(pos):
    half = ROPE_DIM // 2
    inv_freq = ROPE_THETA ** (-jnp.arange(half, dtype=f32) / half)
    ang = pos.astype(f32)[:, None] * inv_freq[None, :]
    cos, sin = jnp.cos(ang), jnp.sin(ang)
    n = pos.shape[0]
    ones = jnp.ones((n, NOPE_DIM), f32)
    tail = jnp.zeros((n, LANE - QK_DIM), f32)
    cos_t = jnp.concatenate([ones, cos, cos, tail], axis=1)
    sin_t = jnp.concatenate([0 * ones, sin, sin, tail], axis=1)
    return cos_t, sin_t


def _rope_partner(w):
    half = ROPE_DIM // 2
    x1, x2 = w[..., -ROPE_DIM:-half], w[..., -half:]
    return jnp.concatenate([jnp.zeros_like(w[..., :-ROPE_DIM]), -x2, x1], axis=-1)


def _layer_weights(l, g_mix, w_in, g_qa, w_q_up, g_qnorm, g_kva, w_kv_up, g_knorm, w_conv, b_conv, b_igate, b_fgate,
                   g_mhead, w_out, g_xattn, g_mem, w_xq, w_xk, w_xv, g_xq, g_xk, w_xo, g_mlp, w_ff1, w_ff2):
    wi = w_in[l]
    off_kr = Q_RANK + KV_RANK
    off_mqk = off_kr + ROPE_DIM
    off_mv = off_mqk + 2 * ML_WIDTH
    off_mi = off_mv + ML_WIDTH
    off_mo = off_mi + 2 * ML_HEADS
    w_kr = wi[:, off_kr:off_mqk]
    packed = jnp.concatenate([
        wi[:, :off_kr],
        _pad_lanes(w_kr, NOPE_DIM),
        wi[:, off_mqk:off_mi],
        wi[:, off_mo:],
        _pad_lanes(wi[:, off_mi:off_mo], 0),
        _pad_lanes(_rope_partner(w_kr), NOPE_DIM),
    ], axis=1)
    wq = w_q_up[l].reshape(Q_RANK, MLA_HEADS, QK_DIM)
    wq = jnp.concatenate([_pad_lanes(wq, 0).reshape(Q_RANK, MLA_HEADS * LANE),
                          _pad_lanes(_rope_partner(wq), 0).reshape(Q_RANK, MLA_HEADS * LANE)], axis=1)
    wkv = w_kv_up[l].reshape(KV_RANK, MLA_HEADS, NOPE_DIM + V_DIM)
    wv = wkv[:, :, NOPE_DIM:].reshape(KV_RANK, MLA_HEADS * V_DIM)
    row = lambda a: a.reshape(1, -1)
    return {
        "g_mix": row(g_mix[l]), "w_in": packed.astype(bf16), "g_qa": row(g_qa[l]),
        "w_q": wq.astype(bf16),
        "g_qn": _pad_lanes(row(g_qnorm[l]), 0), "g_kva": row(g_kva[l]),
        "w_k": _pad_lanes(wkv[:, :, :NOPE_DIM], 0).reshape(KV_RANK, MLA_HEADS * LANE).astype(bf16),
        "w_v": wv.astype(bf16), "w_vT": wv.T.astype(bf16),
        "g_kn": _pad_lanes(row(g_knorm[l]), 0),
        "w_conv": w_conv[l], "b_conv": row(b_conv[l]),
        "b_gate": _pad_lanes(row(jnp.concatenate([b_igate[l], b_fgate[l]])), 0),
        "g_mh": row(g_mhead[l]),
        "w_out_a": w_out[l][:MLA_HEADS * V_DIM].astype(bf16), "w_out_m": w_out[l][MLA_HEADS * V_DIM:].astype(bf16),
        "g_xattn": row(g_xattn[l]), "g_mem": row(g_mem[l]),
        "w_xq": w_xq[l].astype(bf16), "w_xk": w_xk[l].astype(bf16), "w_xv": w_xv[l].astype(bf16),
        "g_xq": row(g_xq[l]), "g_xk": row(g_xk[l]), "w_xo": w_xo[l].astype(bf16),
        "g_mlp": row(g_mlp[l]), "w_ff1": w_ff1[l].astype(bf16), "w_ff2": w_ff2[l].astype(bf16),
    }


def _unpack_state(c_new, n_new, m_new, conv_new):
    return (c_new, n_new, m_new[:, 0, GATE_F:GATE_F + ML_HEADS], conv_new[:, SUBLANE - (CONV_W - 1):, :])


def kernel(x_prompt, x_sample, cache_mla_ckv, cache_mla_krope, state_mlstm_C, state_mlstm_n, state_mlstm_m,
           state_mlstm_conv, cache_mem_k, cache_mem_v, mem_prompt, g_mix, w_in, g_qa, w_q_up, g_qnorm, g_kva,
           w_kv_up, g_knorm, w_conv, b_conv, b_igate, b_fgate, g_mhead, w_out, g_xattn, g_mem, w_xq, w_xk, w_xv,
           g_xq, g_xk, w_xo, g_mlp, w_ff1, w_ff2):
    depth = w_in.shape[0]
    bp, sp, _ = x_prompt.shape
    bs, ss, _ = x_sample.shape
    past = cache_mla_ckv.shape[2]
    n_mem = mem_prompt.shape[1]
    weights = (g_mix, w_in, g_qa, w_q_up, g_qnorm, g_kva, w_kv_up, g_knorm, w_conv, b_conv, b_igate, b_fgate,
               g_mhead, w_out, g_xattn, g_mem, w_xq, w_xk, w_xv, g_xq, g_xk, w_xo, g_mlp, w_ff1, w_ff2)

    tabs_p = _rope_tables(jnp.arange(sp))
    tabs_s = tuple(jnp.tile(t, (bs, 1)) for t in _rope_tables(past + jnp.arange(ss)))
    ml_chunk = min(ML_CHUNK, sp)
    cache_ckv = cache_mla_ckv.reshape(depth * bs, past, KV_RANK)
    cache_kr = _pad_lanes(cache_mla_krope, NOPE_DIM).reshape(depth * bs, past, LANE)
    cache_c = state_mlstm_C.reshape(depth * bs, ML_HEADS, ML_DH, ML_DH)
    cache_mk = cache_mem_k.reshape(depth * bs, n_mem, XA_HEADS, XA_DH)
    cache_mv = cache_mem_v.reshape(depth * bs, n_mem, XA_HEADS, XA_DH)

    xp, xs = x_prompt, x_sample.reshape(1, bs * ss, D_MODEL)
    p_out = [[] for _ in range(8)]
    s_out = [[] for _ in range(6)]
    for l in range(depth):
        W = _layer_weights(l, *weights)

        mem_k, mem_v = _memkv(mem_prompt, W)
        q, ckv, kr, mqk, mv, og, gate, k, vt = _inproj(xp, tabs_p, W, True)
        bound = jnp.max(jnp.abs(g_qnorm[l])) * jnp.max(jnp.abs(g_knorm[l])) * (QK_DIM * MLA_SCALE * LOG2E * 1.05)
        a = lax.cond(bound <= SAFE_LOG2_SCORE, functools.partial(_mla_prompt, bounded=True),
                     functools.partial(_mla_prompt, bounded=False), q, k, vt)
        zeros = lambda *s: jnp.zeros((bp,) + s, f32)
        hm, *state = _mlstm(mqk, mv, gate, zeros(SUBLANE, 2 * ML_WIDTH), zeros(ML_HEADS, ML_DH, ML_DH),
                            zeros(ML_HEADS, ML_DH), zeros(1, LANE), W, ml_chunk)
        xp = _tail(xp, a, hm, og, mem_k, mem_v, W)
        new = (ckv, kr[..., NOPE_DIM:QK_DIM]) + _unpack_state(*state) + (
            mem_k.reshape(bp, n_mem, XA_HEADS, XA_DH), mem_v.reshape(bp, n_mem, XA_HEADS, XA_DH))
        for lst, t in zip(p_out, new):
            lst.append(t)

        q, ckv, kr, mqk, mv, og, gate = _inproj(xs, tabs_s, W, False)
        per_stream = lambda t: t.reshape(bs, ss, t.shape[-1])
        ckv, kr = per_stream(ckv), per_stream(kr)
        kn, vn = _kvup(ckv, kr, W)
        kp, vp = _kvup(cache_ckv, cache_kr, W, first=l * bs, count=bs)
        a = _mla_sample(q, kp, vp, kn, vn)
        conv0 = jnp.pad(state_mlstm_conv[l], ((0, 0), (SUBLANE - (CONV_W - 1), 0), (0, 0)))
        m0 = _pad_lanes(state_mlstm_m[l], GATE_F).reshape(bs, 1, LANE)
        hm, *state = _mlstm(per_stream(mqk), per_stream(mv), per_stream(gate), conv0, cache_c,
                            state_mlstm_n[l], m0, W, ss, c_first=l * bs)
        flat = lambda t: t.reshape(1, bs * ss, t.shape[-1])
        xs = _outproj(xs, flat(a), flat(hm), og, W)
        xs = flat(_xattn(per_stream(xs), cache_mk, cache_mv, W, mem_first=l * bs))
        xs = _mlp(xs, W)
        new = (ckv, kr[..., NOPE_DIM:QK_DIM]) + _unpack_state(*state)
        for lst, t in zip(s_out, new):
            lst.append(t)

    outs_p = tuple(jnp.stack(t) for t in p_out)
    outs_s = tuple(jnp.stack(t) for t in s_out)
    return (xp, xs.reshape(bs, ss, D_MODEL)) + outs_p + outs_s
```
